```python
import functools
import jax, jax.numpy as jnp
from jax import lax
import numpy as np

D_MODEL = 1024
BATCH = 8
SEQ = 2048
DEPTH = 2
DEC_BATCH = 128
DEC_SEQ = 8
PAST_LEN = 16384
PAGE_SIZE = 128

N_AB_LAYERS = (DEPTH + 1) // 2
N_CD_LAYERS = DEPTH // 2
D_A = D_MODEL // 2
CONV_A = 3
N_HEADS_B = 8
NOPE_DIM = 64
ROPE_DIM = 32
V_DIM = 64
Q_RANK = 384
KV_RANK = 256
ROPE_THETA = 10000.0
Q_BLOCK = 128
D_C = D_MODEL // 2
CONV_C = 31
D_D = D_MODEL // 2
G_D = 8
CHUNK = 128
D_FF = ((8 * D_MODEL // 3 + 255) // 256) * 256
D_IN_AB = 3 * D_A + Q_RANK + KV_RANK + ROPE_DIM
D_MIX_AB = D_A + N_HEADS_B * V_DIM
D_IN_CD = 2 * D_C + 2 * D_D
D_MIX_CD = D_C + D_D
EPS = 1e-6
NEG_INF = -1e30
SCALE_B = (NOPE_DIM + ROPE_DIM) ** -0.5

kernel_name = 'hybrid_shortconv_mla_conformer_gmlp_step'


def rmsnorm(x, g):
    xf = x.astype(jnp.float32)
    y = xf * lax.rsqrt(jnp.mean(xf * xf, axis=-1, keepdims=True) + EPS)
    return (y * g.astype(jnp.float32)).astype(x.dtype)


def layernorm(x, g, b):
    xf = x.astype(jnp.float32)
    xc = xf - jnp.mean(xf, axis=-1, keepdims=True)
    y = xc * lax.rsqrt(jnp.mean(xc * xc, axis=-1, keepdims=True) + EPS)
    return (y * g.astype(jnp.float32) + b.astype(jnp.float32)).astype(x.dtype)


def causal_dwconv(x_hist, w):
    return lax.conv_general_dilated(
        x_hist, w[:, None, :].astype(x_hist.dtype), window_strides=(1,), padding='VALID',
        dimension_numbers=('NWC', 'WIO', 'NWC'), feature_group_count=x_hist.shape[-1])


def rope_tables(pos, dtype):
    inv_freq = ROPE_THETA ** (-jnp.arange(0, ROPE_DIM, 2, dtype=jnp.float32) / ROPE_DIM)
    ang = pos.astype(jnp.float32)[:, None] * inv_freq[None, :]
    return jnp.cos(ang).astype(dtype), jnp.sin(ang).astype(dtype)


def apply_rope(x, cos, sin):
    x1, x2 = jnp.split(x, 2, axis=-1)
    return jnp.concatenate([x1 * cos - x2 * sin, x2 * cos + x1 * sin], axis=-1)


def modulation(c, w_mod, b_mod):
    m = jax.nn.silu(c) @ w_mod + b_mod
    return [q[:, None, :] for q in jnp.split(m, 6, axis=-1)]


def swiglu(h, wg, wu, wd):
    return (jax.nn.silu(h @ wg) * (h @ wu)) @ wd


def mla_scores(q_lat, q_pe, ckv, kpe):
    s = jnp.einsum('bhsr,btr->bhst', q_lat, ckv) + jnp.einsum('bhsk,btk->bhst', q_pe, kpe)
    return s.astype(jnp.float32) * SCALE_B


def prompt_attend(q_lat, q_pe, ckv, kpe):
    b, h, s, r = q_lat.shape
    nb = s // Q_BLOCK
    ql = q_lat.reshape(b, h, nb, Q_BLOCK, r).transpose(2, 0, 1, 3, 4)
    qp = q_pe.reshape(b, h, nb, Q_BLOCK, ROPE_DIM).transpose(2, 0, 1, 3, 4)
    key_pos = jnp.arange(s)

    def block(args):
        qlb, qpb, blk = args
        q_pos = blk * Q_BLOCK + jnp.arange(Q_BLOCK)
        sc = jnp.where(key_pos[None, :] <= q_pos[:, None], mla_scores(qlb, qpb, ckv, kpe), NEG_INF)
        p = jax.nn.softmax(sc, axis=-1).astype(ckv.dtype)
        return jnp.einsum('bhst,btr->bhsr', p, ckv)

    o = lax.map(block, (ql, qp, jnp.arange(nb)))
    return o.transpose(1, 2, 0, 3, 4).reshape(b, h, s, r)


def online_update(carry, sc, v):
    m, l, acc = carry
    m_new = jnp.maximum(m, jnp.max(sc, axis=-1))
    alpha = jnp.exp(m - m_new)
    p = jnp.exp(sc - m_new[..., None])
    l_new = l * alpha + jnp.sum(p, axis=-1)
    acc_new = acc * alpha[..., None] + jnp.einsum('bhst,btr->bhsr', p, v.astype(jnp.float32))
    return (m_new, l_new, acc_new)


def paged_attend(q_lat, q_pe, ckv_new, kpe_new, cache_ckv, cache_kpe, page_table, layer):
    b, h, s, r = q_lat.shape
    init = (jnp.full((b, h, s), NEG_INF, jnp.float32),
            jnp.zeros((b, h, s), jnp.float32),
            jnp.zeros((b, h, s, r), jnp.float32))

    def page_step(carry, pages):
        ck = cache_ckv[layer, pages]
        kp = cache_kpe[layer, pages]
        return online_update(carry, mla_scores(q_lat, q_pe, ck, kp), ck), None

    carry, _ = lax.scan(page_step, init, page_table.T)
    causal = jnp.tril(jnp.ones((s, s), dtype=bool))
    sc = jnp.where(causal, mla_scores(q_lat, q_pe, ckv_new, kpe_new), NEG_INF)
    _, l, acc = online_update(carry, sc, ckv_new)
    return (acc / l[..., None]).astype(q_lat.dtype)


def ab_mixer(h, pos, conv_hist, attend, w_in, conv_w, q_norm, w_q_up, kv_norm, w_kv_up, w_out):
    b, s, _ = h.shape
    proj = h @ w_in
    gate_out, gate_in, xa, cq, ckv, kpe = jnp.split(
        proj, [D_A, 2 * D_A, 3 * D_A, 3 * D_A + Q_RANK, 3 * D_A + Q_RANK + KV_RANK], axis=-1)
    z_hist = jnp.concatenate([conv_hist, gate_in * xa], axis=1)
    ya = gate_out * causal_dwconv(z_hist, conv_w)
    new_conv = z_hist[:, -(CONV_A - 1):]
    cos, sin = rope_tables(pos, h.dtype)
    q = (rmsnorm(cq, q_norm) @ w_q_up).reshape(b, s, N_HEADS_B, NOPE_DIM + ROPE_DIM)
    q_nope = q[..., :NOPE_DIM]
    q_pe = apply_rope(q[..., NOPE_DIM:], cos[:, None, :], sin[:, None, :]).transpose(0, 2, 1, 3)
    ckv = rmsnorm(ckv, kv_norm)
    kpe = apply_rope(kpe, cos, sin)
    w_kv = w_kv_up.reshape(KV_RANK, N_HEADS_B, NOPE_DIM + V_DIM)
    w_uk, w_uv = w_kv[..., :NOPE_DIM], w_kv[..., NOPE_DIM:]
    q_lat = jnp.einsum('bshn,rhn->bhsr', q_nope, w_uk)
    o_lat = attend(q_lat, q_pe, ckv, kpe)
    yb = jnp.einsum('bhsr,rhv->bshv', o_lat, w_uv).reshape(b, s, N_HEADS_B * V_DIM)
    y = jnp.concatenate([ya, yb], axis=-1) @ w_out
    return y, new_conv, ckv, kpe


def chunk_spatial(u, v, w_sp, b_sp, n):
    b, s, _ = v.shape
    wm = jnp.tril(w_sp[:, :n, :n])
    vg = v.reshape(b, s // n, n, G_D, D_D // G_D)
    sv = jnp.einsum('gts,bcsgd->bctgd', wm, vg) + b_sp[:, :n].T[None, None, :, :, None]
    return u * sv.reshape(b, s, D_D)


def cd_mixer(h, conv_hist, chunk_rows, w_in, conv_w, conv_b, ln_c_g, ln_c_b, ln_v_g, ln_v_b,
             w_sp, b_sp, w_out):
    proj = h @ w_in
    a, g, u, v = jnp.split(proj, [D_C, 2 * D_C, 2 * D_C + D_D], axis=-1)
    hist = jnp.concatenate([conv_hist, a * jax.nn.sigmoid(g)], axis=1)
    yc = causal_dwconv(hist, conv_w) + conv_b
    yc = jax.nn.silu(layernorm(yc, ln_c_g, ln_c_b))
    new_conv = hist[:, -(CONV_C - 1):]
    v = layernorm(v, ln_v_g, ln_v_b)
    yd = chunk_spatial(u, v, w_sp, b_sp, chunk_rows)
    y = jnp.concatenate([yc, yd], axis=-1) @ w_out
    return y, new_conv, v


def trunk(x, c, pos, hist_a, hist_c, attend, chunk_rows, p):
    conv_a, ckvs, kpes, conv_c, vds = [], [], [], [], []
    for l in range(DEPTH):
        sh_m, sc_m, g_m, sh_f, sc_f, g_f = modulation(c, p['w_mod'][l], p['b_mod'][l])
        h = rmsnorm(x, p['norm_mix_pre'][l]) * (1 + sc_m) + sh_m
        i = l // 2
        if l % 2 == 0:
            y, na, ckv, kpe = ab_mixer(
                h, pos, hist_a[i], functools.partial(attend, i), p['w_in_ab'][i], p['conv_a_w'][i],
                p['q_norm'][i], p['w_q_up'][i], p['kv_norm'][i], p['w_kv_up'][i], p['w_out_ab'][i])
            conv_a.append(na)
            ckvs.append(ckv)
            kpes.append(kpe)
        else:
            y, nc, vd = cd_mixer(
                h, hist_c[i], chunk_rows, p['w_in_cd'][i], p['conv_c_w'][i], p['conv_c_b'][i],
                p['ln_c_g'][i], p['ln_c_b'][i], p['ln_v_g'][i], p['ln_v_b'][i],
                p['w_spatial'][i], p['b_spatial'][i], p['w_out_cd'][i])
            conv_c.append(nc)
            vds.append(vd)
        x = x + g_m * rmsnorm(y, p['norm_mix_post'][l])
        h = rmsnorm(x, p['norm_ffn_pre'][l]) * (1 + sc_f) + sh_f
        f = swiglu(h, p['w_ffn_gate'][l], p['w_ffn_up'][l], p['w_ffn_down'][l])
        x = x + g_f * rmsnorm(f, p['norm_ffn_post'][l])
    return x, jnp.stack(conv_a), jnp.stack(ckvs), jnp.stack(kpes), jnp.stack(conv_c), jnp.stack(vds)


def setup_inputs(seed: int = 0) -> dict:
    key = jax.random.key(seed)

    def nrm(i, shape, scale):
        return jax.random.normal(jax.random.fold_in(key, i), shape, jnp.float32) * scale

    n_pages = PAST_LEN // PAGE_SIZE
    n_phys = (DEC_BATCH * n_pages * 5) // 4
    perm = jax.random.permutation(jax.random.fold_in(key, 1000), n_phys)
    page_table = perm[: DEC_BATCH * n_pages].reshape(DEC_BATCH, n_pages).astype(jnp.int32)
    na, nc, d = N_AB_LAYERS, N_CD_LAYERS, D_MODEL
    return {
        'x_prompt': nrm(0, (BATCH, SEQ, d), 1.0),
        'x_sample': nrm(1, (DEC_BATCH, DEC_SEQ, d), 1.0),
        'cache_ckv': nrm(2, (na, n_phys, PAGE_SIZE, KV_RANK), 1.0),
        'cache_kpe': nrm(3, (na, n_phys, PAGE_SIZE, ROPE_DIM), 1.0),
        'state_conv_a': nrm(4, (na, DEC_BATCH, CONV_A - 1, D_A), 1.0),
        'state_conv_c': nrm(5, (nc, DEC_BATCH, CONV_C - 1, D_C), 0.5),
        'page_table': page_table,
        'c_prompt': nrm(6, (BATCH, d), 1.0),
        'c_sample': nrm(7, (DEC_BATCH, d), 1.0),
        'w_mod': nrm(8, (DEPTH, d, 6 * d), d ** -0.5),
        'b_mod': nrm(9, (DEPTH, 6 * d), 0.01),
        'norm_mix_pre': 1.0 + nrm(10, (DEPTH, d), 0.05),
        'norm_mix_post': 1.0 + nrm(11, (DEPTH, d), 0.05),
        'norm_ffn_pre': 1.0 + nrm(12, (DEPTH, d), 0.05),
        'norm_ffn_post': 1.0 + nrm(13, (DEPTH, d), 0.05),
        'w_in_ab': nrm(14, (na, d, D_IN_AB), d ** -0.5),
        'conv_a_w': nrm(15, (na, CONV_A, D_A), CONV_A ** -0.5),
        'q_norm': 1.0 + nrm(16, (na, Q_RANK), 0.05),
        'w_q_up': nrm(17, (na, Q_RANK, N_HEADS_B * (NOPE_DIM + ROPE_DIM)), Q_RANK ** -0.5),
        'kv_norm': 1.0 + nrm(18, (na, KV_RANK), 0.05),
        'w_kv_up': nrm(19, (na, KV_RANK, N_HEADS_B * (NOPE_DIM + V_DIM)), KV_RANK ** -0.5),
        'w_out_ab': nrm(20, (na, D_MIX_AB, d), D_MIX_AB ** -0.5),
        'w_in_cd': nrm(21, (nc, d, D_IN_CD), d ** -0.5),
        'conv_c_w': nrm(22, (nc, CONV_C, D_C), CONV_C ** -0.5),
        'conv_c_b': nrm(23, (nc, D_C), 0.01),
        'ln_c_g': 1.0 + nrm(24, (nc, D_C), 0.05),
        'ln_c_b': nrm(25, (nc, D_C), 0.01),
        'ln_v_g': 1.0 + nrm(26, (nc, D_D), 0.05),
        'ln_v_b': nrm(27, (nc, D_D), 0.01),
        'w_spatial': nrm(28, (nc, G_D, CHUNK, CHUNK), CHUNK ** -0.5),
        'b_spatial': 1.0 + nrm(29, (nc, G_D, CHUNK), 0.1),
        'w_out_cd': nrm(30, (nc, D_MIX_CD, d), D_MIX_CD ** -0.5),
        'w_ffn_gate': nrm(31, (DEPTH, d, D_FF), d ** -0.5),
        'w_ffn_up': nrm(32, (DEPTH, d, D_FF), d ** -0.5),
        'w_ffn_down': nrm(33, (DEPTH, D_FF, d), D_FF ** -0.5),
    }


def reference(x_prompt, x_sample, cache_ckv, cache_kpe, state_conv_a, state_conv_c, page_table,
              c_prompt, c_sample, w_mod, b_mod, norm_mix_pre, norm_mix_post, norm_ffn_pre,
              norm_ffn_post, w_in_ab, conv_a_w, q_norm, w_q_up, kv_norm, w_kv_up, w_out_ab,
              w_in_cd, conv_c_w, conv_c_b, ln_c_g, ln_c_b, ln_v_g, ln_v_b, w_spatial, b_spatial,
              w_out_cd, w_ffn_gate, w_ffn_up, w_ffn_down):
    p = {
        'w_mod': w_mod, 'b_mod': b_mod, 'norm_mix_pre': norm_mix_pre,
        'norm_mix_post': norm_mix_post, 'norm_ffn_pre': norm_ffn_pre,
        'norm_ffn_post': norm_ffn_post, 'w_in_ab': w_in_ab, 'conv_a_w': conv_a_w,
        'q_norm': q_norm, 'w_q_up': w_q_up, 'kv_norm': kv_norm, 'w_kv_up': w_kv_up,
        'w_out_ab': w_out_ab, 'w_in_cd': w_in_cd, 'conv_c_w': conv_c_w, 'conv_c_b': conv_c_b,
        'ln_c_g': ln_c_g, 'ln_c_b': ln_c_b, 'ln_v_g': ln_v_g, 'ln_v_b': ln_v_b,
        'w_spatial': w_spatial, 'b_spatial': b_spatial, 'w_out_cd': w_out_cd,
        'w_ffn_gate': w_ffn_gate, 'w_ffn_up': w_ffn_up, 'w_ffn_down': w_ffn_down,
    }
    b_p, s_p, _ = x_prompt.shape
    zeros_a = jnp.zeros((N_AB_LAYERS, b_p, CONV_A - 1, D_A), x_prompt.dtype)
    zeros_c = jnp.zeros((N_CD_LAYERS, b_p, CONV_C - 1, D_C), x_prompt.dtype)
    y_prompt, sa_p, ckv_p, kpe_p, sc_p, _ = trunk(
        x_prompt, c_prompt, jnp.arange(s_p), zeros_a, zeros_c,
        lambda i, ql, qp, ck, kp: prompt_attend(ql, qp, ck, kp), CHUNK, p)
    past_len = page_table.shape[1] * PAGE_SIZE
    s_s = x_sample.shape[1]
    y_sample, sa_s, ckv_s, kpe_s, sc_s, vd_s = trunk(
        x_sample, c_sample, past_len + jnp.arange(s_s), state_conv_a, state_conv_c,
        lambda i, ql, qp, ck, kp: paged_attend(ql, qp, ck, kp, cache_ckv, cache_kpe, page_table, i),
        s_s, p)
    return (y_prompt, y_sample, sa_p, ckv_p, kpe_p, sc_p, sa_s, ckv_s, kpe_s, sc_s, vd_s)
```

```python
import functools

import jax
import jax.numpy as jnp
from jax import lax
from jax.experimental import pallas as pl
from jax.experimental.pallas import tpu as pltpu

N_HEADS_B = 8
NOPE_DIM = 64
ROPE_DIM = 32
V_DIM = 64
ROPE_THETA = 10000.0
G_D = 8
CHUNK = 128
PAGE_SIZE = 128
EPS = 1e-6
NEG_INF = -1e30
SCALE_B = (NOPE_DIM + ROPE_DIM) ** -0.5

LANES = 128
SUBLANES = 8
VMEM_LIMIT = 56 * 1024 * 1024

ROW_TILE = 512
ATTN_Q_TILE = 256
PAGES_PER_STEP = 8

F32 = jnp.float32
BF16 = jnp.bfloat16


def _rms(x, g):
    return x * lax.rsqrt(jnp.mean(x * x, axis=-1, keepdims=True) + EPS) * g


def _ln(x, g, b):
    xc = x - jnp.mean(x, axis=-1, keepdims=True)
    return xc * lax.rsqrt(jnp.mean(xc * xc, axis=-1, keepdims=True) + EPS) * g + b


def _silu(x):
    return x * jax.nn.sigmoid(x)


def _dot(a, b):
    return jnp.dot(a.astype(BF16), b.astype(BF16), preferred_element_type=F32)


def _dot_nt(a, b):
    return lax.dot_general(a.astype(BF16), b.astype(BF16), (((1,), (1,)), ((), ())),
                           preferred_element_type=F32)


def _params(*sem):
    return pltpu.CompilerParams(dimension_semantics=sem, vmem_limit_bytes=VMEM_LIMIT)


def _mod_kernel(c_ref, w_ref, b_ref, o_ref):
    o_ref[0] = _dot(_silu(c_ref[...]), w_ref[0]) + b_ref[0]


def _modulation(c_all, w_mod, b_mod):
    n_l, d, n = w_mod.shape
    bc = c_all.shape[0]
    tn = n // 4
    return pl.pallas_call(
        _mod_kernel,
        grid=(n_l, n // tn),
        in_specs=[pl.BlockSpec((bc, d), lambda l, j: (0, 0)),
                  pl.BlockSpec((1, d, tn), lambda l, j: (l, 0, j)),
                  pl.BlockSpec((1, 1, tn), lambda l, j: (l, 0, j))],
        out_specs=pl.BlockSpec((1, bc, tn), lambda l, j: (l, 0, j)),
        out_shape=jax.ShapeDtypeStruct((n_l, bc, n), F32),
        compiler_params=_params("arbitrary", "arbitrary"),
    )(c_all, w_mod, b_mod.reshape(n_l, 1, n))


def _mod_spec(bt, d, k, grid_rank):
    if grid_rank == 2:
        return pl.BlockSpec((bt, 1, d), lambda b, s: (b, 0, k))
    return pl.BlockSpec((bt, 1, d), lambda b, s, f: (b, 0, k))


def _const_spec(shape, grid_rank):
    zeros = (0,) * len(shape)
    if grid_rank == 2:
        return pl.BlockSpec(shape, lambda b, s: zeros)
    return pl.BlockSpec(shape, lambda b, s, f: zeros)


def _tiles(b, s):
    if s >= ROW_TILE:
        assert s % ROW_TILE == 0
        return 1, ROW_TILE
    assert ROW_TILE % s == 0 and s % SUBLANES == 0 and b % (ROW_TILE // s) == 0
    return ROW_TILE // s, s


def _ab_in_kernel(x_ref, sh_ref, sc_ref, g_ref, w_ref, cw_ref, hist_ref, qn_ref, wq_ref, kvn_ref,
                  wuk_ref, cosq_ref, sinq_ref, ropek_ref,
                  ya_ref, ql_ref, qpe_ref, ckv_ref, ckvb_ref, kpe_ref, kpeb_ref, nconv_ref,
                  zs_ref, *, bt, ts, d_a, q_rank, kv_rank):
    tm = bt * ts
    hist_rows = nconv_ref.shape[1]
    x = x_ref[...]
    h = _rms(x, g_ref[...]) * (1.0 + sc_ref[...]) + sh_ref[...]
    proj = _dot(h.reshape(tm, h.shape[-1]), w_ref[...])
    o_cq = 3 * d_a
    o_ckv = o_cq + q_rank
    o_kpe = o_ckv + kv_rank
    gate_out = proj[:, 0:d_a].reshape(bt, ts, d_a)
    z = (proj[:, d_a:2 * d_a] * proj[:, 2 * d_a:3 * d_a]).reshape(bt, ts, d_a)

    base = SUBLANES - hist_rows

    @pl.when(pl.program_id(1) == 0)
    def _():
        zs_ref[:, base:SUBLANES, :] = hist_ref[...]

    zs_ref[:, SUBLANES:SUBLANES + ts, :] = z
    cw = cw_ref[...]
    conv = z * cw[hist_rows:hist_rows + 1, :]
    for k in range(hist_rows):
        conv = conv + zs_ref[:, base + k:base + k + ts, :] * cw[k:k + 1, :]
    ya_ref[...] = (gate_out * conv).reshape(tm, d_a).astype(ya_ref.dtype)
    tail = zs_ref[:, ts + base:ts + SUBLANES, :]
    nconv_ref[...] = tail
    zs_ref[:, base:SUBLANES, :] = tail

    cqn = _rms(proj[:, o_cq:o_ckv], qn_ref[...])
    q = _dot(cqn, wq_ref[...]) * SCALE_B
    n_nope = N_HEADS_B * NOPE_DIM
    half = N_HEADS_B * ROPE_DIM // 2
    r1 = q[:, n_nope:n_nope + half].reshape(bt, ts, half)
    r2 = q[:, n_nope + half:n_nope + 2 * half].reshape(bt, ts, half)
    cosq = cosq_ref[...][None]
    sinq = sinq_ref[...][None]
    qpe = jnp.concatenate([r1 * cosq - r2 * sinq, r2 * cosq + r1 * sinq], axis=-1)
    qpe_ref[...] = qpe.reshape(tm, 2 * half).astype(qpe_ref.dtype)
    for hh in range(N_HEADS_B):
        pair = q[:, LANES * (hh // 2):LANES * (hh // 2 + 1)]
        ql_ref[hh] = _dot(pair, wuk_ref[hh]).astype(ql_ref.dtype)

    ckvn = _rms(proj[:, o_ckv:o_kpe], kvn_ref[...])
    ckv_ref[...] = ckvn
    ckvb_ref[...] = ckvn.astype(BF16)
    kt = proj[:, o_kpe:o_kpe + LANES].reshape(bt, ts, LANES) * ropek_ref[...][None]
    kpe = (kt[:, :, 0:ROPE_DIM] + kt[:, :, ROPE_DIM:2 * ROPE_DIM]).reshape(tm, ROPE_DIM)
    kpe_ref[...] = kpe
    kpeb_ref[...] = kpe.astype(BF16)


def _ab_in(x, mod3, g_pre, w_in_ext, conv_w, hist, q_norm, w_q_perm, kv_norm, w_uk_pad,
           cosq, sinq, ropek, ql_dtype):
    b, s, d = x.shape
    m = b * s
    bt, ts = _tiles(b, s)
    tm = bt * ts
    d_a = conv_w.shape[1]
    q_rank = q_norm.shape[-1]
    kv_rank = kv_norm.shape[-1]
    hist_rows = hist.shape[1]
    n_ext = w_in_ext.shape[1]
    grid = (b // bt, s // ts)
    row = lambda bb, ss: (bb * (s // ts) + ss, 0)
    kern = functools.partial(_ab_in_kernel, bt=bt, ts=ts, d_a=d_a, q_rank=q_rank, kv_rank=kv_rank)
    out_shape = (
        jax.ShapeDtypeStruct((m, d_a), BF16),
        jax.ShapeDtypeStruct((N_HEADS_B, m, kv_rank), ql_dtype),
        jax.ShapeDtypeStruct((m, N_HEADS_B * ROPE_DIM), ql_dtype),
        jax.ShapeDtypeStruct((m, kv_rank), F32),
        jax.ShapeDtypeStruct((m, kv_rank), BF16),
        jax.ShapeDtypeStruct((m, ROPE_DIM), F32),
        jax.ShapeDtypeStruct((m, ROPE_DIM), BF16),
        jax.ShapeDtypeStruct((b, hist_rows, d_a), F32),
    )
    out_specs = (
        pl.BlockSpec((tm, d_a), row),
        pl.BlockSpec((N_HEADS_B, tm, kv_rank), lambda bb, ss: (0, bb * (s // ts) + ss, 0)),
        pl.BlockSpec((tm, N_HEADS_B * ROPE_DIM), row),
        pl.BlockSpec((tm, kv_rank), row),
        pl.BlockSpec((tm, kv_rank), row),
        pl.BlockSpec((tm, ROPE_DIM), row),
        pl.BlockSpec((tm, ROPE_DIM), row),
        pl.BlockSpec((bt, hist_rows, d_a), lambda bb, ss: (bb, 0, 0)),
    )
    in_specs = [
        pl.BlockSpec((bt, ts, d), lambda bb, ss: (bb, ss, 0)),
        _mod_spec(bt, d, 0, 2), _mod_spec(bt, d, 1, 2),
        _const_spec((1, d), 2),
        _const_spec((d, n_ext), 2),
        _const_spec(conv_w.shape, 2),
        pl.BlockSpec((bt, hist_rows, d_a), lambda bb, ss: (bb, 0, 0)),
        _const_spec((1, q_rank), 2),
        _const_spec(w_q_perm.shape, 2),
        _const_spec((1, kv_rank), 2),
        _const_spec(w_uk_pad.shape, 2),
        pl.BlockSpec((ts, LANES), lambda bb, ss: (ss, 0)),
        pl.BlockSpec((ts, LANES), lambda bb, ss: (ss, 0)),
        pl.BlockSpec((ts, LANES), lambda bb, ss: (ss, 0)),
    ]
    return pl.pallas_call(
        kern, grid=grid, in_specs=in_specs, out_specs=out_specs, out_shape=out_shape,
        scratch_shapes=[pltpu.VMEM((bt, ts + SUBLANES, d_a), F32)],
        compiler_params=_params("arbitrary", "arbitrary"),
    )(x, mod3, mod3, g_pre.reshape(1, d), w_in_ext, conv_w, hist, q_norm.reshape(1, q_rank),
      w_q_perm, kv_norm.reshape(1, kv_rank), w_uk_pad, cosq, sinq, ropek)


def _softmax_step(s, k_bf, m_ref, l_ref, acc_ref):
    m_old = m_ref[...]
    m_new = jnp.maximum(m_old, jnp.max(s, axis=-1, keepdims=True))
    alpha = jnp.exp(m_old - m_new)
    p = jnp.exp(s - m_new)
    l_ref[...] = alpha * l_ref[...] + jnp.sum(p, axis=-1, keepdims=True)
    acc_ref[...] = alpha * acc_ref[...] + jnp.dot(p.astype(BF16), k_bf, preferred_element_type=F32)
    m_ref[...] = m_new


def _value_up(o, wuv_ref, rows):
    parts = []
    for j in range(N_HEADS_B // 2):
        o0 = o[(2 * j) * rows:(2 * j + 1) * rows]
        o1 = o[(2 * j + 1) * rows:(2 * j + 2) * rows]
        parts.append(_dot(o0, wuv_ref[2 * j]) + _dot(o1, wuv_ref[2 * j + 1]))
    return jnp.concatenate(parts, axis=-1)


def _prompt_attn_kernel(ql_ref, qp_ref, ckv_ref, kpe_ref, wuv_ref, yb_ref, m_ref, l_ref, acc_ref, *, tq):
    i = pl.program_id(1)
    rows = N_HEADS_B * tq
    q = ql_ref[...].reshape(rows, ql_ref.shape[-1])
    qp = qp_ref[...].reshape(rows, qp_ref.shape[-1])
    m_ref[...] = jnp.full(m_ref.shape, NEG_INF, F32)
    l_ref[...] = jnp.zeros(l_ref.shape, F32)
    acc_ref[...] = jnp.zeros(acc_ref.shape, F32)

    def scores(j):
        start = pl.multiple_of(j * tq, tq)
        k = ckv_ref[pl.ds(start, tq), :]
        kp = kpe_ref[pl.ds(start, tq), :]
        return _dot_nt(q, k) + _dot_nt(qp, kp), k

    def body(j, carry):
        s, k = scores(j)
        _softmax_step(s, k, m_ref, l_ref, acc_ref)
        return carry

    lax.fori_loop(0, i, body, 0)
    s, k = scores(i)
    qpos = lax.broadcasted_iota(jnp.int32, s.shape, 0) & (tq - 1)
    kpos = lax.broadcasted_iota(jnp.int32, s.shape, 1)
    s = jnp.where(kpos <= qpos, s, NEG_INF)
    _softmax_step(s, k, m_ref, l_ref, acc_ref)
    o = acc_ref[...] / l_ref[...]
    yb_ref[...] = _value_up(o, wuv_ref, tq).astype(yb_ref.dtype)


def _prompt_attention(q_lat, q_pe, ckv_bf, kpe_bf, w_uv_pad, b, s):
    n_h, m, r = q_lat.shape
    tq = ATTN_Q_TILE
    nq = s // tq
    kern = functools.partial(_prompt_attn_kernel, tq=tq)
    return pl.pallas_call(
        kern, grid=(b, nq),
        in_specs=[pl.BlockSpec((n_h, tq, r), lambda bb, i: (0, bb * nq + i, 0)),
                  pl.BlockSpec((n_h, tq, ROPE_DIM), lambda bb, i: (0, bb * nq + i, 0)),
                  pl.BlockSpec((s, r), lambda bb, i: (bb, 0)),
                  pl.BlockSpec((s, ROPE_DIM), lambda bb, i: (bb, 0)),
                  _const_spec(w_uv_pad.shape, 2)],
        out_specs=pl.BlockSpec((tq, n_h * V_DIM), lambda bb, i: (bb * nq + i, 0)),
        out_shape=jax.ShapeDtypeStruct((m, n_h * V_DIM), BF16),
        scratch_shapes=[pltpu.VMEM((n_h * tq, 1), F32), pltpu.VMEM((n_h * tq, 1), F32),
                        pltpu.VMEM((n_h * tq, r), F32)],
        compiler_params=_params("arbitrary", "arbitrary"),
    )(q_lat, q_pe, ckv_bf, kpe_bf, w_uv_pad)


def _paged_attn_kernel(pt_ref, ql_ref, qp_ref, *rest, n_pages_step, s_new):
    del pt_ref
    ck_refs = rest[:n_pages_step]
    kp_refs = rest[n_pages_step:2 * n_pages_step]
    cnew_ref, knew_ref, wuv_ref, yb_ref, m_ref, l_ref, acc_ref = rest[2 * n_pages_step:]
    c = pl.program_id(1)
    rows = N_HEADS_B * s_new
    q = ql_ref[...].reshape(rows, ql_ref.shape[-1]).astype(BF16)
    qp = qp_ref[...].reshape(rows, qp_ref.shape[-1]).astype(BF16)

    @pl.when(c == 0)
    def _():
        m_ref[...] = jnp.full(m_ref.shape, NEG_INF, F32)
        l_ref[...] = jnp.zeros(l_ref.shape, F32)
        acc_ref[...] = jnp.zeros(acc_ref.shape, F32)

    ks = [r[0, 0].astype(BF16) for r in ck_refs]
    ss = [_dot_nt(q, k) + _dot_nt(qp, r[0, 0]) for k, r in zip(ks, kp_refs)]
    m_old = m_ref[...]
    m_new = m_old
    for s in ss:
        m_new = jnp.maximum(m_new, jnp.max(s, axis=-1, keepdims=True))
    alpha = jnp.exp(m_old - m_new)
    l_new = alpha * l_ref[...]
    acc = alpha * acc_ref[...]
    for s, k in zip(ss, ks):
        p = jnp.exp(s - m_new)
        l_new = l_new + jnp.sum(p, axis=-1, keepdims=True)
        acc = acc + jnp.dot(p.astype(BF16), k, preferred_element_type=F32)
    m_ref[...] = m_new
    l_ref[...] = l_new
    acc_ref[...] = acc

    @pl.when(c == pl.num_programs(1) - 1)
    def _():
        k = cnew_ref[...].astype(BF16)
        s = _dot_nt(q, k) + _dot_nt(qp, knew_ref[...])
        qpos = lax.broadcasted_iota(jnp.int32, s.shape, 0) & (s_new - 1)
        kpos = lax.broadcasted_iota(jnp.int32, s.shape, 1)
        s = jnp.where(kpos <= qpos, s, NEG_INF)
        _softmax_step(s, k, m_ref, l_ref, acc_ref)
        o = acc_ref[...] / l_ref[...]
        yb_ref[...] = _value_up(o, wuv_ref, s_new).astype(yb_ref.dtype)


def _paged_attention(q_lat, q_pe, ckv_new, kpe_new, cache_ckv, cache_kpe, page_table, layer, w_uv_pad, s_new):
    n_h, m, r = q_lat.shape
    b, n_pages = page_table.shape
    assert s_new & (s_new - 1) == 0 and n_pages % PAGES_PER_STEP == 0
    pps = PAGES_PER_STEP
    page = cache_ckv.shape[2]

    def page_spec(width, i):
        return pl.BlockSpec((1, 1, page, width), lambda bb, c, pt: (layer, pt[bb, c * pps + i], 0, 0))

    in_specs = ([pl.BlockSpec((n_h, s_new, r), lambda bb, c, pt: (0, bb, 0)),
                 pl.BlockSpec((n_h, s_new, ROPE_DIM), lambda bb, c, pt: (0, bb, 0))]
                + [page_spec(r, i) for i in range(pps)]
                + [page_spec(ROPE_DIM, i) for i in range(pps)]
                + [pl.BlockSpec((s_new, r), lambda bb, c, pt: (bb, 0)),
                   pl.BlockSpec((s_new, ROPE_DIM), lambda bb, c, pt: (bb, 0)),
                   pl.BlockSpec(w_uv_pad.shape, lambda bb, c, pt: (0, 0, 0))])
    kern = functools.partial(_paged_attn_kernel, n_pages_step=pps, s_new=s_new)
    grid_spec = pltpu.PrefetchScalarGridSpec(
        num_scalar_prefetch=1, grid=(b, n_pages // pps), in_specs=in_specs,
        out_specs=pl.BlockSpec((s_new, n_h * V_DIM), lambda bb, c, pt: (bb, 0)),
        scratch_shapes=[pltpu.VMEM((n_h * s_new, 1), F32), pltpu.VMEM((n_h * s_new, 1), F32),
                        pltpu.VMEM((n_h * s_new, r), F32)])
    return pl.pallas_call(
        kern, grid_spec=grid_spec,
        out_shape=jax.ShapeDtypeStruct((m, n_h * V_DIM), F32),
        compiler_params=_params("arbitrary", "arbitrary"),
    )(page_table, q_lat, q_pe, *([cache_ckv] * pps), *([cache_kpe] * pps), ckv_new, kpe_new, w_uv_pad)


def _cd_in_kernel(x_ref, sh_ref, sc_ref, g_ref, w_ref, cw_ref, cb_ref, lcg_ref, lcb_ref, lvg_ref, lvb_ref,
                  hist_ref, wsp_ref, bsp_ref, *rest, bt, ts, d_c, with_v):
    if with_v:
        yc_ref, yd_ref, nconv_ref, vn_ref, hs_ref = rest
    else:
        yc_ref, yd_ref, nconv_ref, hs_ref = rest
    tm = bt * ts
    hist_rows = nconv_ref.shape[1]
    n_taps = hist_rows + 1
    pad = hs_ref.shape[1] - ts
    base = pad - hist_rows
    x = x_ref[...]
    h = _rms(x, g_ref[...]) * (1.0 + sc_ref[...]) + sh_ref[...]
    proj = _dot(h.reshape(tm, h.shape[-1]), w_ref[...])

    glu = (proj[:, 0:d_c] * jax.nn.sigmoid(proj[:, d_c:2 * d_c])).reshape(bt, ts, d_c)

    @pl.when(pl.program_id(1) == 0)
    def _():
        hs_ref[:, base:pad, :] = hist_ref[...]

    hs_ref[:, pad:pad + ts, :] = glu
    cw = cw_ref[...]
    conv = glu * cw[n_taps - 1:n_taps, :] + cb_ref[...]
    for k in range(n_taps - 1):
        conv = conv + hs_ref[:, base + k:base + k + ts, :] * cw[k:k + 1, :]
    yc = _silu(_ln(conv, lcg_ref[...], lcb_ref[...]))
    yc_ref[...] = yc.reshape(tm, d_c).astype(yc_ref.dtype)
    tail = hs_ref[:, ts + base:ts + pad, :]
    nconv_ref[...] = tail
    hs_ref[:, base:pad, :] = tail

    u = proj[:, 2 * d_c:3 * d_c]
    vn = _ln(proj[:, 3 * d_c:4 * d_c], lvg_ref[...], lvb_ref[...])
    if with_v:
        vn_ref[...] = vn
    low = lax.broadcasted_iota(jnp.int32, (CHUNK, LANES), 1) < (LANES // 2)
    bsp = bsp_ref[...]
    for c in range(tm // CHUNK):
        parts = []
        for j in range(d_c // LANES):
            vj = vn[c * CHUNK:(c + 1) * CHUNK, j * LANES:(j + 1) * LANES]
            rhs = jnp.concatenate([jnp.where(low, vj, 0.0), jnp.where(low, 0.0, vj)], axis=0)
            parts.append(_dot(wsp_ref[j], rhs))
        sv = jnp.concatenate(parts, axis=-1) + bsp
        yd_ref[c * CHUNK:(c + 1) * CHUNK, :] = (u[c * CHUNK:(c + 1) * CHUNK] * sv).astype(yd_ref.dtype)


def _cd_in(x, mod3, g_pre, w_in, conv_w, conv_b, ln_c_g, ln_c_b, ln_v_g, ln_v_b, hist, w_sp_cat, b_sp_rows,
           with_v):
    b, s, d = x.shape
    m = b * s
    bt, ts = _tiles(b, s)
    tm = bt * ts
    d_c = conv_w.shape[1]
    hist_rows = hist.shape[1]
    pad = -(-hist_rows // SUBLANES) * SUBLANES
    grid = (b // bt, s // ts)
    row = lambda bb, ss: (bb * (s // ts) + ss, 0)
    vec = lambda a: a.reshape(1, -1)
    kern = functools.partial(_cd_in_kernel, bt=bt, ts=ts, d_c=d_c, with_v=with_v)
    out_shape = [jax.ShapeDtypeStruct((m, d_c), BF16), jax.ShapeDtypeStruct((m, d_c), BF16),
                 jax.ShapeDtypeStruct((b, hist_rows, d_c), F32)]
    out_specs = [pl.BlockSpec((tm, d_c), row), pl.BlockSpec((tm, d_c), row),
                 pl.BlockSpec((bt, hist_rows, d_c), lambda bb, ss: (bb, 0, 0))]
    if with_v:
        out_shape.append(jax.ShapeDtypeStruct((m, d_c), F32))
        out_specs.append(pl.BlockSpec((tm, d_c), row))
    in_specs = [
        pl.BlockSpec((bt, ts, d), lambda bb, ss: (bb, ss, 0)),
        _mod_spec(bt, d, 0, 2), _mod_spec(bt, d, 1, 2),
        _const_spec((1, d), 2),
        _const_spec(w_in.shape, 2),
        _const_spec(conv_w.shape, 2),
        _const_spec((1, d_c), 2), _const_spec((1, d_c), 2), _const_spec((1, d_c), 2),
        _const_spec((1, d_c), 2), _const_spec((1, d_c), 2),
        pl.BlockSpec((bt, hist_rows, d_c), lambda bb, ss: (bb, 0, 0)),
        _const_spec(w_sp_cat.shape, 2),
        _const_spec(b_sp_rows.shape, 2),
    ]
    return pl.pallas_call(
        kern, grid=grid, in_specs=in_specs, out_specs=tuple(out_specs), out_shape=tuple(out_shape),
        scratch_shapes=[pltpu.VMEM((bt, ts + pad, d_c), F32)],
        compiler_params=_params("arbitrary", "arbitrary"),
    )(x, mod3, mod3, vec(g_pre), w_in, conv_w, vec(conv_b), vec(ln_c_g), vec(ln_c_b), vec(ln_v_g),
      vec(ln_v_b), hist, w_sp_cat, b_sp_rows)


def _out_ffn_kernel(y0_ref, y1_ref, x_ref, gm_ref, shf_ref, scf_ref, gf_ref, npost_ref, nfpre_ref, nfpost_ref,
                    wo_ref, wg_ref, wu_ref, wd_ref, o_ref, x1_ref, h_ref, acc_ref, *, bt, ts):
    f = pl.program_id(2)
    tm = bt * ts
    d = x_ref.shape[-1]
    k0 = y0_ref.shape[-1]

    @pl.when(f == 0)
    def _():
        y = _dot(y0_ref[...], wo_ref[0:k0, :]) + _dot(y1_ref[...], wo_ref[k0:, :])
        x1 = x_ref[...] + gm_ref[...] * _rms(y, npost_ref[...]).reshape(bt, ts, d)
        x1_ref[...] = x1
        h = _rms(x1, nfpre_ref[...]) * (1.0 + scf_ref[...]) + shf_ref[...]
        h_ref[...] = h.reshape(tm, d).astype(BF16)
        acc_ref[...] = jnp.zeros(acc_ref.shape, F32)

    h = h_ref[...]
    a = _silu(_dot(h, wg_ref[...])) * _dot(h, wu_ref[...])
    acc_ref[...] += _dot(a, wd_ref[...])

    @pl.when(f == pl.num_programs(2) - 1)
    def _():
        o_ref[...] = x1_ref[...] + gf_ref[...] * _rms(acc_ref[...], nfpost_ref[...]).reshape(bt, ts, d)


def _out_ffn(y0, y1, x, mod3, norm_post, norm_ffn_pre, norm_ffn_post, w_out, w_gate, w_up, w_down):
    b, s, d = x.shape
    bt, ts = _tiles(b, s)
    tm = bt * ts
    d_ff = w_gate.shape[1]
    n_f = 2
    tf = d_ff // n_f
    assert tf % LANES == 0
    grid = (b // bt, s // ts, n_f)
    row = lambda bb, ss, f: (bb * (s // ts) + ss, 0)
    vec = lambda a: a.reshape(1, -1)
    kern = functools.partial(_out_ffn_kernel, bt=bt, ts=ts)
    in_specs = [
        pl.BlockSpec((tm, y0.shape[1]), row),
        pl.BlockSpec((tm, y1.shape[1]), row),
        pl.BlockSpec((bt, ts, d), lambda bb, ss, f: (bb, ss, 0)),
        _mod_spec(bt, d, 2, 3), _mod_spec(bt, d, 3, 3), _mod_spec(bt, d, 4, 3), _mod_spec(bt, d, 5, 3),
        _const_spec((1, d), 3), _const_spec((1, d), 3), _const_spec((1, d), 3),
        _const_spec(w_out.shape, 3),
        pl.BlockSpec((d, tf), lambda bb, ss, f: (0, f)),
        pl.BlockSpec((d, tf), lambda bb, ss, f: (0, f)),
        pl.BlockSpec((tf, d), lambda bb, ss, f: (f, 0)),
    ]
    return pl.pallas_call(
        kern, grid=grid, in_specs=in_specs,
        out_specs=pl.BlockSpec((bt, ts, d), lambda bb, ss, f: (bb, ss, 0)),
        out_shape=jax.ShapeDtypeStruct((b, s, d), F32),
        scratch_shapes=[pltpu.VMEM((bt, ts, d), F32), pltpu.VMEM((tm, d), BF16), pltpu.VMEM((tm, d), F32)],
        compiler_params=_params("arbitrary", "arbitrary", "arbitrary"),
    )(y0, y1, x, mod3, mod3, mod3, mod3, vec(norm_post), vec(norm_ffn_pre), vec(norm_ffn_post),
      w_out, w_gate, w_up, w_down)


def _rope_tables(pos):
    inv_freq = ROPE_THETA ** (-jnp.arange(0, ROPE_DIM, 2, dtype=F32) / ROPE_DIM)
    ang = pos.astype(F32)[:, None] * inv_freq[None, :]
    cos, sin = jnp.cos(ang), jnp.sin(ang)
    cosq = jnp.tile(cos, (1, N_HEADS_B))
    sinq = jnp.tile(sin, (1, N_HEADS_B))
    zeros = jnp.zeros((pos.shape[0], LANES - 2 * ROPE_DIM), F32)
    ropek = jnp.concatenate([cos, cos, -sin, sin, zeros], axis=-1)
    return cosq, sinq, ropek


def _prep_ab_weights(w_in, w_q_up, w_kv_up, d_a, q_rank, kv_rank):
    d = w_in.shape[0]
    o_kpe = 3 * d_a + q_rank + kv_rank
    half = ROPE_DIM // 2
    swapped = jnp.concatenate([w_in[:, o_kpe + half:o_kpe + ROPE_DIM], w_in[:, o_kpe:o_kpe + half]], axis=1)
    w_in_ext = jnp.concatenate(
        [w_in, swapped, jnp.zeros((d, LANES - 2 * ROPE_DIM), w_in.dtype)], axis=1).astype(BF16)
    wq = w_q_up.reshape(q_rank, N_HEADS_B, NOPE_DIM + ROPE_DIM)
    w_q_perm = jnp.concatenate([
        wq[:, :, :NOPE_DIM].reshape(q_rank, -1),
        wq[:, :, NOPE_DIM:NOPE_DIM + half].reshape(q_rank, -1),
        wq[:, :, NOPE_DIM + half:].reshape(q_rank, -1)], axis=1).astype(BF16)
    w_kv = w_kv_up.reshape(kv_rank, N_HEADS_B, NOPE_DIM + V_DIM)
    w_uk_t = jnp.transpose(w_kv[..., :NOPE_DIM], (1, 2, 0))
    w_uv = jnp.transpose(w_kv[..., NOPE_DIM:], (1, 0, 2))
    odd = (jnp.arange(N_HEADS_B) % 2 == 1)
    zk = jnp.zeros_like(w_uk_t)
    w_uk_pad = jnp.where(odd[:, None, None],
                         jnp.concatenate([zk, w_uk_t], axis=1),
                         jnp.concatenate([w_uk_t, zk], axis=1)).astype(BF16)
    zv = jnp.zeros_like(w_uv)
    w_uv_pad = jnp.where(odd[:, None, None],
                         jnp.concatenate([zv, w_uv], axis=2),
                         jnp.concatenate([w_uv, zv], axis=2)).astype(BF16)
    return w_in_ext, w_q_perm, w_uk_pad, w_uv_pad


def _prep_spatial(w_sp, b_sp, n):
    wm = jnp.tril(w_sp[:, :n, :n])
    reps = CHUNK // n
    eye = jnp.eye(reps, dtype=w_sp.dtype)
    big = jnp.einsum('ab,gts->gatbs', eye, wm).reshape(G_D, CHUNK, CHUNK)
    w_cat = jnp.concatenate([big[0::2], big[1::2]], axis=2).astype(BF16)
    b_rows = jnp.tile(b_sp[:, :n].T, (reps, 1))
    return w_cat, b_rows


def _trunk(x, mods, pos, hist_a, hist_c, attend, chunk_rows, with_v, ql_dtype, p):
    b, s, d = x.shape
    depth = p['w_ffn_gate'].shape[0]
    cosq, sinq, ropek = _rope_tables(pos)
    conv_a, ckvs, kpes, conv_c, vds = [], [], [], [], []
    for l in range(depth):
        mod3 = mods[l].reshape(b, 1, -1)
        i = l // 2
        if l % 2 == 0:
            w_in_ext, w_q_perm, w_uk_pad, w_uv_pad = p['ab'][i]
            ya, q_lat, q_pe, ckv, ckv_bf, kpe, kpe_bf, nconv = _ab_in(
                x, mod3, p['norm_mix_pre'][l], w_in_ext, p['conv_a_w'][i], hist_a[i], p['q_norm'][i],
                w_q_perm, p['kv_norm'][i], w_uk_pad, cosq, sinq, ropek, ql_dtype)
            m = b * s
            half = ROPE_DIM // 2
            q_pe = jnp.transpose(q_pe.reshape(m, 2, N_HEADS_B, half), (2, 0, 1, 3)).reshape(
                N_HEADS_B, m, ROPE_DIM)
            y1 = attend(i, q_lat, q_pe, ckv, ckv_bf, kpe, kpe_bf, w_uv_pad)
            y0 = ya
            w_out = p['w_out_ab'][i]
            conv_a.append(nconv)
            ckvs.append(ckv.reshape(b, s, -1))
            kpes.append(kpe.reshape(b, s, -1))
        else:
            d_d = p['w_spatial'].shape
            w_sp_cat, b_rows = _prep_spatial(p['w_spatial'][i], p['b_spatial'][i], chunk_rows)
            d_c = p['conv_c_w'].shape[-1]
            b_sp_rows = jnp.repeat(b_rows, d_c // G_D, axis=1)
            outs = _cd_in(x, mod3, p['norm_mix_pre'][l], p['w_in_cd'][i], p['conv_c_w'][i], p['conv_c_b'][i],
                          p['ln_c_g'][i], p['ln_c_b'][i], p['ln_v_g'][i], p['ln_v_b'][i], hist_c[i],
                          w_sp_cat, b_sp_rows, with_v)
            y0, y1, nconv = outs[:3]
            if with_v:
                vds.append(outs[3].reshape(b, s, -1))
            w_out = p['w_out_cd'][i]
            conv_c.append(nconv)
        x = _out_ffn(y0, y1, x, mod3, p['norm_mix_post'][l], p['norm_ffn_pre'][l], p['norm_ffn_post'][l],
                     w_out, p['w_ffn_gate'][l], p['w_ffn_up'][l], p['w_ffn_down'][l])
    return x, conv_a, ckvs, kpes, conv_c, vds


def kernel(x_prompt, x_sample, cache_ckv, cache_kpe, state_conv_a, state_conv_c, page_table, c_prompt, c_sample, w_mod, b_mod, norm_mix_pre, norm_mix_post, norm_ffn_pre, norm_ffn_post, w_in_ab, conv_a_w, q_norm, w_q_up, kv_norm, w_kv_up, w_out_ab, w_in_cd, conv_c_w, conv_c_b, ln_c_g, ln_c_b, ln_v_g, ln_v_b, w_spatial, b_spatial, w_out_cd, w_ffn_gate, w_ffn_up, w_ffn_down):
    b_p, s_p, d = x_prompt.shape
    b_s, s_s, _ = x_sample.shape
    n_ab, n_cd = w_in_ab.shape[0], w_in_cd.shape[0]
    d_a = conv_a_w.shape[-1]
    d_c = conv_c_w.shape[-1]
    q_rank, kv_rank = q_norm.shape[-1], kv_norm.shape[-1]

    p = {
        'norm_mix_pre': norm_mix_pre, 'norm_mix_post': norm_mix_post, 'norm_ffn_pre': norm_ffn_pre,
        'norm_ffn_post': norm_ffn_post, 'conv_a_w': conv_a_w, 'q_norm': q_norm, 'kv_norm': kv_norm,
        'w_out_ab': w_out_ab.astype(BF16), 'w_in_cd': w_in_cd.astype(BF16), 'conv_c_w': conv_c_w,
        'conv_c_b': conv_c_b, 'ln_c_g': ln_c_g, 'ln_c_b': ln_c_b, 'ln_v_g': ln_v_g, 'ln_v_b': ln_v_b,
        'w_spatial': w_spatial, 'b_spatial': b_spatial, 'w_out_cd': w_out_cd.astype(BF16),
        'w_ffn_gate': w_ffn_gate.astype(BF16), 'w_ffn_up': w_ffn_up.astype(BF16),
        'w_ffn_down': w_ffn_down.astype(BF16),
        'ab': [_prep_ab_weights(w_in_ab[i], w_q_up[i], w_kv_up[i], d_a, q_rank, kv_rank) for i in range(n_ab)],
    }

    mods = _modulation(jnp.concatenate([c_prompt, c_sample], axis=0), w_mod, b_mod)
    mods_p, mods_s = mods[:, :b_p], mods[:, b_p:]

    zeros_a = jnp.zeros((n_ab, b_p, state_conv_a.shape[2], d_a), F32)
    zeros_c = jnp.zeros((n_cd, b_p, state_conv_c.shape[2], d_c), F32)

    def attend_prompt(i, q_lat, q_pe, ckv, ckv_bf, kpe, kpe_bf, w_uv_pad):
        return _prompt_attention(q_lat, q_pe, ckv_bf, kpe_bf, w_uv_pad, b_p, s_p)

    y_p, sa_p, ckv_p, kpe_p, sc_p, _ = _trunk(
        x_prompt, mods_p, jnp.arange(s_p), zeros_a, zeros_c, attend_prompt, CHUNK, False, BF16, p)

    past_len = page_table.shape[1] * PAGE_SIZE

    def attend_sample(i, q_lat, q_pe, ckv, ckv_bf, kpe, kpe_bf, w_uv_pad):
        return _paged_attention(q_lat, q_pe, ckv, kpe, cache_ckv, cache_kpe, page_table, i, w_uv_pad, s_s)

    y_s, sa_s, ckv_s, kpe_s, sc_s, vd_s = _trunk(
        x_sample, mods_s, past_len + jnp.arange(s_s), state_conv_a, state_conv_c, attend_sample, s_s,
        True, F32, p)

    st = jnp.stack
    return (y_p, y_s, st(sa_p), st(ckv_p), st(kpe_p), st(sc_p), st(sa_s), st(ckv_s), st(kpe_s), st(sc_s),
            st(vd_s))
```

```python
import functools

import jax
import jax.numpy as jnp
from jax import lax
from jax.experimental import pallas as pl
from jax.experimental.pallas import tpu as pltpu

N_HEADS_B = 8
NOPE_DIM = 64
ROPE_DIM = 32
V_DIM = 64
ROPE_THETA = 10000.0
G_D = 8
CHUNK = 128
PAGE_SIZE = 128
EPS = 1e-6
NEG_INF = -1e30
SCALE_B = (NOPE_DIM + ROPE_DIM) ** -0.5

LANES = 128
SUBLANES = 8
VMEM_LIMIT = 56 * 1024 * 1024

ROW_TILE = 512
ATTN_Q_TILE = 256
PAGES_PER_CHUNK = 16

F32 = jnp.float32
BF16 = jnp.bfloat16


def _rms(x, g):
    return x * lax.rsqrt(jnp.mean(x * x, axis=-1, keepdims=True) + EPS) * g


def _ln(x, g, b):
    xc = x - jnp.mean(x, axis=-1, keepdims=True)
    return xc * lax.rsqrt(jnp.mean(xc * xc, axis=-1, keepdims=True) + EPS) * g + b


def _silu(x):
    return x * jax.nn.sigmoid(x)


def _dot(a, b):
    return jnp.dot(a.astype(BF16), b.astype(BF16), preferred_element_type=F32)


def _dot_nt(a, b):
    return lax.dot_general(a.astype(BF16), b.astype(BF16), (((1,), (1,)), ((), ())),
                           preferred_element_type=F32)


def _params(*sem):
    return pltpu.CompilerParams(dimension_semantics=sem, vmem_limit_bytes=VMEM_LIMIT)


def _mod_kernel(c_ref, w_ref, b_ref, o_ref):
    o_ref[0] = _dot(_silu(c_ref[...]), w_ref[0]) + b_ref[0]


def _modulation(c_all, w_mod, b_mod):
    n_l, d, n = w_mod.shape
    bc = c_all.shape[0]
    tn = n // 4
    return pl.pallas_call(
        _mod_kernel,
        grid=(n_l, n // tn),
        in_specs=[pl.BlockSpec((bc, d), lambda l, j: (0, 0)),
                  pl.BlockSpec((1, d, tn), lambda l, j: (l, 0, j)),
                  pl.BlockSpec((1, 1, tn), lambda l, j: (l, 0, j))],
        out_specs=pl.BlockSpec((1, bc, tn), lambda l, j: (l, 0, j)),
        out_shape=jax.ShapeDtypeStruct((n_l, bc, n), F32),
        compiler_params=_params("arbitrary", "arbitrary"),
    )(c_all, w_mod, b_mod.reshape(n_l, 1, n))


def _mod_spec(bt, d, k, grid_rank):
    if grid_rank == 2:
        return pl.BlockSpec((bt, 1, d), lambda b, s: (b, 0, k))
    return pl.BlockSpec((bt, 1, d), lambda b, s, f: (b, 0, k))


def _const_spec(shape, grid_rank):
    zeros = (0,) * len(shape)
    if grid_rank == 2:
        return pl.BlockSpec(shape, lambda b, s: zeros)
    return pl.BlockSpec(shape, lambda b, s, f: zeros)


def _tiles(b, s):
    if s >= ROW_TILE:
        assert s % ROW_TILE == 0
        return 1, ROW_TILE
    assert ROW_TILE % s == 0 and s % SUBLANES == 0 and b % (ROW_TILE // s) == 0
    return ROW_TILE // s, s


def _ab_in_kernel(x_ref, sh_ref, sc_ref, g_ref, w_ref, cw_ref, hist_ref, qn_ref, wq_ref, kvn_ref,
                  wuk_ref, cosq_ref, sinq_ref, ropek_ref,
                  ya_ref, qcat_ref, ckv_ref, kpe_ref, kcat_ref, nconv_ref,
                  zs_ref, *, bt, ts, d_a, q_rank, kv_rank):
    tm = bt * ts
    hist_rows = nconv_ref.shape[1]
    x = x_ref[...]
    h = _rms(x, g_ref[...]) * (1.0 + sc_ref[...]) + sh_ref[...]
    proj = _dot(h.reshape(tm, h.shape[-1]), w_ref[...])
    o_cq = 3 * d_a
    o_ckv = o_cq + q_rank
    o_kpe = o_ckv + kv_rank
    gate_out = proj[:, 0:d_a].reshape(bt, ts, d_a)
    z = (proj[:, d_a:2 * d_a] * proj[:, 2 * d_a:3 * d_a]).reshape(bt, ts, d_a)

    base = SUBLANES - hist_rows

    @pl.when(pl.program_id(1) == 0)
    def _():
        zs_ref[:, base:SUBLANES, :] = hist_ref[...]

    zs_ref[:, SUBLANES:SUBLANES + ts, :] = z
    cw = cw_ref[...]
    conv = z * cw[hist_rows:hist_rows + 1, :]
    for k in range(hist_rows):
        conv = conv + zs_ref[:, base + k:base + k + ts, :] * cw[k:k + 1, :]
    ya_ref[...] = (gate_out * conv).reshape(tm, d_a).astype(ya_ref.dtype)
    tail = zs_ref[:, ts + base:ts + SUBLANES, :]
    nconv_ref[...] = tail
    zs_ref[:, base:SUBLANES, :] = tail

    cqn = _rms(proj[:, o_cq:o_ckv], qn_ref[...])
    q = _dot(cqn, wq_ref[...]) * SCALE_B
    n_nope = N_HEADS_B * NOPE_DIM
    n_rope = N_HEADS_B * ROPE_DIM
    qr = q[:, n_nope:n_nope + n_rope].reshape(bt, ts, n_rope)
    qs = q[:, n_nope + n_rope:n_nope + 2 * n_rope].reshape(bt, ts, n_rope)
    qpe = (qr * cosq_ref[...][None] + qs * sinq_ref[...][None]).reshape(tm, n_rope)
    for hh in range(N_HEADS_B):
        pair = q[:, LANES * (hh // 2):LANES * (hh // 2 + 1)]
        q_lat = _dot(pair, wuk_ref[hh])
        qcat_ref[hh] = jnp.concatenate(
            [q_lat, qpe[:, hh * ROPE_DIM:(hh + 1) * ROPE_DIM]], axis=-1).astype(qcat_ref.dtype)

    ckvn = _rms(proj[:, o_ckv:o_kpe], kvn_ref[...])
    ckv_ref[...] = ckvn
    kt = proj[:, o_kpe:o_kpe + LANES].reshape(bt, ts, LANES) * ropek_ref[...][None]
    kpe = (kt[:, :, 0:ROPE_DIM] + kt[:, :, ROPE_DIM:2 * ROPE_DIM]).reshape(tm, ROPE_DIM)
    kpe_ref[...] = kpe
    kcat_ref[...] = jnp.concatenate([ckvn, kpe], axis=-1).astype(BF16)


def _ab_in(x, mod3, g_pre, w_in_ext, conv_w, hist, q_norm, w_q_perm, kv_norm, w_uk_pad,
           cosq, sinq, ropek, ql_dtype):
    b, s, d = x.shape
    m = b * s
    bt, ts = _tiles(b, s)
    tm = bt * ts
    d_a = conv_w.shape[1]
    q_rank = q_norm.shape[-1]
    kv_rank = kv_norm.shape[-1]
    hist_rows = hist.shape[1]
    n_ext = w_in_ext.shape[1]
    grid = (b // bt, s // ts)
    row = lambda bb, ss: (bb * (s // ts) + ss, 0)
    kern = functools.partial(_ab_in_kernel, bt=bt, ts=ts, d_a=d_a, q_rank=q_rank, kv_rank=kv_rank)
    d_qk = kv_rank + ROPE_DIM
    out_shape = (
        jax.ShapeDtypeStruct((m, d_a), BF16),
        jax.ShapeDtypeStruct((N_HEADS_B, m, d_qk), ql_dtype),
        jax.ShapeDtypeStruct((m, kv_rank), F32),
        jax.ShapeDtypeStruct((m, ROPE_DIM), F32),
        jax.ShapeDtypeStruct((m, d_qk), BF16),
        jax.ShapeDtypeStruct((b, hist_rows, d_a), F32),
    )
    out_specs = (
        pl.BlockSpec((tm, d_a), row),
        pl.BlockSpec((N_HEADS_B, tm, d_qk), lambda bb, ss: (0, bb * (s // ts) + ss, 0)),
        pl.BlockSpec((tm, kv_rank), row),
        pl.BlockSpec((tm, ROPE_DIM), row),
        pl.BlockSpec((tm, d_qk), row),
        pl.BlockSpec((bt, hist_rows, d_a), lambda bb, ss: (bb, 0, 0)),
    )
    in_specs = [
        pl.BlockSpec((bt, ts, d), lambda bb, ss: (bb, ss, 0)),
        _mod_spec(bt, d, 0, 2), _mod_spec(bt, d, 1, 2),
        _const_spec((1, d), 2),
        _const_spec((d, n_ext), 2),
        _const_spec(conv_w.shape, 2),
        pl.BlockSpec((bt, hist_rows, d_a), lambda bb, ss: (bb, 0, 0)),
        _const_spec((1, q_rank), 2),
        _const_spec(w_q_perm.shape, 2),
        _const_spec((1, kv_rank), 2),
        _const_spec(w_uk_pad.shape, 2),
        pl.BlockSpec((ts, cosq.shape[1]), lambda bb, ss: (ss, 0)),
        pl.BlockSpec((ts, sinq.shape[1]), lambda bb, ss: (ss, 0)),
        pl.BlockSpec((ts, LANES), lambda bb, ss: (ss, 0)),
    ]
    return pl.pallas_call(
        kern, grid=grid, in_specs=in_specs, out_specs=out_specs, out_shape=out_shape,
        scratch_shapes=[pltpu.VMEM((bt, ts + SUBLANES, d_a), F32)],
        compiler_params=_params("arbitrary", "arbitrary"),
    )(x, mod3, mod3, g_pre.reshape(1, d), w_in_ext, conv_w, hist, q_norm.reshape(1, q_rank),
      w_q_perm, kv_norm.reshape(1, kv_rank), w_uk_pad, cosq, sinq, ropek)


def _fold_lanes(x, op):
    out = x[:, 0:LANES]
    for t in range(1, x.shape[1] // LANES):
        out = op(out, x[:, t * LANES:(t + 1) * LANES])
    return out


def _value_up(o, wuv_ref, rows):
    parts = []
    for j in range(N_HEADS_B // 2):
        o0 = o[(2 * j) * rows:(2 * j + 1) * rows]
        o1 = o[(2 * j + 1) * rows:(2 * j + 2) * rows]
        parts.append(_dot(o0, wuv_ref[2 * j]) + _dot(o1, wuv_ref[2 * j + 1]))
    return jnp.concatenate(parts, axis=-1)


def _prompt_attn_kernel(q_ref, k_ref, wuv_ref, yb_ref, s_ref, stat_ref, lsum_ref, acc_ref, *, tq, r):
    i = pl.program_id(1)
    rows = N_HEADS_B * tq
    q = q_ref[...].reshape(rows, q_ref.shape[-1])

    def key_block(j):
        return k_ref[pl.ds(pl.multiple_of(j * tq, tq), tq), :]

    def scores_pass(j, mp):
        s = _dot_nt(q, key_block(j))
        s_ref[j] = s
        return jnp.maximum(mp, _fold_lanes(s, jnp.maximum))

    stat_ref[...] = jnp.full(stat_ref.shape, NEG_INF, F32)

    def scores_body(j, carry):
        stat_ref[...] = scores_pass(j, stat_ref[...])
        return carry

    lax.fori_loop(0, i, scores_body, 0)
    s = _dot_nt(q, key_block(i))
    qpos = lax.broadcasted_iota(jnp.int32, s.shape, 0) & (tq - 1)
    kpos = lax.broadcasted_iota(jnp.int32, s.shape, 1)
    s = jnp.where(kpos <= qpos, s, NEG_INF)
    s_ref[i] = s
    mp = jnp.maximum(stat_ref[...], _fold_lanes(s, jnp.maximum))
    m = jnp.max(mp, axis=-1, keepdims=True)
    stat_ref[...] = jnp.broadcast_to(m, stat_ref.shape)
    acc_ref[...] = jnp.zeros(acc_ref.shape, F32)

    lsum_ref[...] = jnp.zeros(lsum_ref.shape, F32)

    def value_body(j, carry):
        mrow = stat_ref[...]
        p = jnp.exp(s_ref[j] - jnp.concatenate([mrow] * (tq // LANES), axis=-1))
        acc_ref[...] += jnp.dot(p.astype(BF16), key_block(j)[:, 0:r], preferred_element_type=F32)
        lsum_ref[...] += _fold_lanes(p, jnp.add)
        return carry

    lax.fori_loop(0, i + 1, value_body, 0)
    o = acc_ref[...] / jnp.sum(lsum_ref[...], axis=-1, keepdims=True)
    yb_ref[...] = _value_up(o, wuv_ref, tq).astype(yb_ref.dtype)


def _prompt_attention(q_cat, k_cat, w_uv_pad, b, s):
    n_h, m, d_qk = q_cat.shape
    r = d_qk - ROPE_DIM
    tq = ATTN_Q_TILE
    nq = s // tq
    rows = n_h * tq
    kern = functools.partial(_prompt_attn_kernel, tq=tq, r=r)
    return pl.pallas_call(
        kern, grid=(b, nq),
        in_specs=[pl.BlockSpec((n_h, tq, d_qk), lambda bb, i: (0, bb * nq + i, 0)),
                  pl.BlockSpec((s, d_qk), lambda bb, i: (bb, 0)),
                  _const_spec(w_uv_pad.shape, 2)],
        out_specs=pl.BlockSpec((tq, n_h * V_DIM), lambda bb, i: (bb * nq + i, 0)),
        out_shape=jax.ShapeDtypeStruct((m, n_h * V_DIM), BF16),
        scratch_shapes=[pltpu.VMEM((nq, rows, tq), F32), pltpu.VMEM((rows, LANES), F32),
                        pltpu.VMEM((rows, LANES), F32), pltpu.VMEM((rows, r), F32)],
        compiler_params=_params("arbitrary", "arbitrary"),
    )(q_cat, k_cat, w_uv_pad)


def _paged_attn_kernel(pt_ref, q_ref, cnew_ref, knew_ref, wuv_ref, ckv_hbm, kpet_hbm, yb_ref,
                       raw_k, raw_p, kb_ref, s_ref, sems, *, layer, n_chunks, pages_chunk, page, s_new, r):
    b = pl.program_id(0)
    keys_chunk = pages_chunk * page
    rows = N_HEADS_B * s_new
    qc = q_ref[...].reshape(rows, q_ref.shape[-1])
    q_lat = qc[:, 0:r].astype(BF16)
    q_pe = qc[:, r:].astype(BF16)

    def chunk_copies(seq, c, slot):
        copies = []
        for i in range(pages_chunk):
            pg = pt_ref[seq, c * pages_chunk + i]
            copies.append(pltpu.make_async_copy(
                ckv_hbm.at[layer, pg], raw_k.at[slot, pl.ds(i * page, page)], sems.at[0, slot]))
            copies.append(pltpu.make_async_copy(
                kpet_hbm.at[layer, pg], raw_p.at[slot, :, pl.ds(i * page, page)], sems.at[1, slot]))
        return copies

    @pl.when(b == 0)
    def _():
        for cp in chunk_copies(b, 0, 0):
            cp.start()

    mp = jnp.full((rows, LANES), NEG_INF, F32)
    for c in range(n_chunks):
        slot = c % 2
        for cp in chunk_copies(b, c, slot):
            cp.wait()
        if c + 1 < n_chunks:
            for cp in chunk_copies(b, c + 1, 1 - slot):
                cp.start()
        else:
            @pl.when(b + 1 < pl.num_programs(0))
            def _():
                for cp in chunk_copies(b + 1, 0, 1 - slot):
                    cp.start()
        k = raw_k[slot].astype(BF16)
        kb_ref[c * keys_chunk:(c + 1) * keys_chunk, :] = k
        s = _dot_nt(q_lat, k) + jnp.dot(q_pe, raw_p[slot].astype(BF16), preferred_element_type=F32)
        s_ref[:, c * keys_chunk:(c + 1) * keys_chunk] = s
        mp = jnp.maximum(mp, _fold_lanes(s, jnp.maximum))

    k_new = cnew_ref[...].astype(BF16)
    s_n = _dot_nt(q_lat, k_new) + _dot_nt(q_pe, knew_ref[...])
    qpos = lax.broadcasted_iota(jnp.int32, s_n.shape, 0) & (s_new - 1)
    kpos = lax.broadcasted_iota(jnp.int32, s_n.shape, 1)
    s_n = jnp.where(kpos <= qpos, s_n, NEG_INF)
    m = jnp.maximum(jnp.max(mp, axis=-1, keepdims=True), jnp.max(s_n, axis=-1, keepdims=True))

    p_n = jnp.exp(s_n - m)
    acc = jnp.dot(p_n.astype(BF16), k_new, preferred_element_type=F32)
    lp = jnp.zeros((rows, LANES), F32)
    for c in range(n_chunks):
        p = jnp.exp(s_ref[:, c * keys_chunk:(c + 1) * keys_chunk] - m)
        lp = lp + _fold_lanes(p, jnp.add)
        acc = acc + jnp.dot(p.astype(BF16), kb_ref[c * keys_chunk:(c + 1) * keys_chunk, :],
                            preferred_element_type=F32)
    l = jnp.sum(lp, axis=-1, keepdims=True) + jnp.sum(p_n, axis=-1, keepdims=True)
    yb_ref[...] = _value_up(acc / l, wuv_ref, s_new).astype(yb_ref.dtype)


def _paged_attention(q_cat, ckv_new, kpe_new, cache_ckv, cache_kpe_t, page_table, layer, w_uv_pad, s_new):
    n_h, m, d_qk = q_cat.shape
    r = d_qk - ROPE_DIM
    b, n_pages = page_table.shape
    page = cache_ckv.shape[2]
    pages_chunk = min(PAGES_PER_CHUNK, n_pages)
    n_chunks = n_pages // pages_chunk
    assert s_new & (s_new - 1) == 0 and n_pages % pages_chunk == 0 and n_chunks % 2 == 0
    keys_chunk = pages_chunk * page
    rows = n_h * s_new
    kern = functools.partial(_paged_attn_kernel, layer=layer, n_chunks=n_chunks, pages_chunk=pages_chunk,
                             page=page, s_new=s_new, r=r)
    grid_spec = pltpu.PrefetchScalarGridSpec(
        num_scalar_prefetch=1, grid=(b,),
        in_specs=[pl.BlockSpec((n_h, s_new, d_qk), lambda bb, pt: (0, bb, 0)),
                  pl.BlockSpec((s_new, r), lambda bb, pt: (bb, 0)),
                  pl.BlockSpec((s_new, ROPE_DIM), lambda bb, pt: (bb, 0)),
                  pl.BlockSpec(w_uv_pad.shape, lambda bb, pt: (0, 0, 0)),
                  pl.BlockSpec(memory_space=pl.ANY),
                  pl.BlockSpec(memory_space=pl.ANY)],
        out_specs=pl.BlockSpec((s_new, n_h * V_DIM), lambda bb, pt: (bb, 0)),
        scratch_shapes=[pltpu.VMEM((2, keys_chunk, r), F32),
                        pltpu.VMEM((2, ROPE_DIM, keys_chunk), F32),
                        pltpu.VMEM((n_pages * page, r), BF16),
                        pltpu.VMEM((rows, n_pages * page), F32),
                        pltpu.SemaphoreType.DMA((2, 2))])
    return pl.pallas_call(
        kern, grid_spec=grid_spec,
        out_shape=jax.ShapeDtypeStruct((m, n_h * V_DIM), F32),
        compiler_params=_params("arbitrary"),
    )(page_table, q_cat, ckv_new, kpe_new, w_uv_pad, cache_ckv, cache_kpe_t)


def _cd_in_kernel(x_ref, sh_ref, sc_ref, g_ref, w_ref, cw_ref, cb_ref, lcg_ref, lcb_ref, lvg_ref, lvb_ref,
                  hist_ref, wsp_ref, bsp_ref, *rest, bt, ts, d_c, with_v):
    if with_v:
        yc_ref, yd_ref, nconv_ref, vn_ref, hs_ref = rest
    else:
        yc_ref, yd_ref, nconv_ref, hs_ref = rest
    tm = bt * ts
    hist_rows = nconv_ref.shape[1]
    n_taps = hist_rows + 1
    pad = hs_ref.shape[1] - ts
    base = pad - hist_rows
    x = x_ref[...]
    h = _rms(x, g_ref[...]) * (1.0 + sc_ref[...]) + sh_ref[...]
    proj = _dot(h.reshape(tm, h.shape[-1]), w_ref[...])

    glu = (proj[:, 0:d_c] * jax.nn.sigmoid(proj[:, d_c:2 * d_c])).reshape(bt, ts, d_c)

    @pl.when(pl.program_id(1) == 0)
    def _():
        hs_ref[:, base:pad, :] = hist_ref[...]

    hs_ref[:, pad:pad + ts, :] = glu
    cw = cw_ref[...]
    conv = glu * cw[n_taps - 1:n_taps, :] + cb_ref[...]
    for k in range(n_taps - 1):
        conv = conv + hs_ref[:, base + k:base + k + ts, :] * cw[k:k + 1, :]
    yc = _silu(_ln(conv, lcg_ref[...], lcb_ref[...]))
    yc_ref[...] = yc.reshape(tm, d_c).astype(yc_ref.dtype)
    tail = hs_ref[:, ts + base:ts + pad, :]
    nconv_ref[...] = tail
    hs_ref[:, base:pad, :] = tail

    u = proj[:, 2 * d_c:3 * d_c]
    vn = _ln(proj[:, 3 * d_c:4 * d_c], lvg_ref[...], lvb_ref[...])
    if with_v:
        vn_ref[...] = vn
    low = lax.broadcasted_iota(jnp.int32, (CHUNK, LANES), 1) < (LANES // 2)
    bsp = bsp_ref[...]
    for c in range(tm // CHUNK):
        parts = []
        for j in range(d_c // LANES):
            vj = vn[c * CHUNK:(c + 1) * CHUNK, j * LANES:(j + 1) * LANES]
            rhs = jnp.concatenate([jnp.where(low, vj, 0.0), jnp.where(low, 0.0, vj)], axis=0)
            parts.append(_dot(wsp_ref[j], rhs))
        sv = jnp.concatenate(parts, axis=-1) + bsp
        yd_ref[c * CHUNK:(c + 1) * CHUNK, :] = (u[c * CHUNK:(c + 1) * CHUNK] * sv).astype(yd_ref.dtype)


def _cd_in(x, mod3, g_pre, w_in, conv_w, conv_b, ln_c_g, ln_c_b, ln_v_g, ln_v_b, hist, w_sp_cat, b_sp_rows,
           with_v):
    b, s, d = x.shape
    m = b * s
    bt, ts = _tiles(b, s)
    tm = bt * ts
    d_c = conv_w.shape[1]
    hist_rows = hist.shape[1]
    pad = -(-hist_rows // SUBLANES) * SUBLANES
    grid = (b // bt, s // ts)
    row = lambda bb, ss: (bb * (s // ts) + ss, 0)
    vec = lambda a: a.reshape(1, -1)
    kern = functools.partial(_cd_in_kernel, bt=bt, ts=ts, d_c=d_c, with_v=with_v)
    out_shape = [jax.ShapeDtypeStruct((m, d_c), BF16), jax.ShapeDtypeStruct((m, d_c), BF16),
                 jax.ShapeDtypeStruct((b, hist_rows, d_c), F32)]
    out_specs = [pl.BlockSpec((tm, d_c), row), pl.BlockSpec((tm, d_c), row),
                 pl.BlockSpec((bt, hist_rows, d_c), lambda bb, ss: (bb, 0, 0))]
    if with_v:
        out_shape.append(jax.ShapeDtypeStruct((m, d_c), F32))
        out_specs.append(pl.BlockSpec((tm, d_c), row))
    in_specs = [
        pl.BlockSpec((bt, ts, d), lambda bb, ss: (bb, ss, 0)),
        _mod_spec(bt, d, 0, 2), _mod_spec(bt, d, 1, 2),
        _const_spec((1, d), 2),
        _const_spec(w_in.shape, 2),
        _const_spec(conv_w.shape, 2),
        _const_spec((1, d_c), 2), _const_spec((1, d_c), 2), _const_spec((1, d_c), 2),
        _const_spec((1, d_c), 2), _const_spec((1, d_c), 2),
        pl.BlockSpec((bt, hist_rows, d_c), lambda bb, ss: (bb, 0, 0)),
        _const_spec(w_sp_cat.shape, 2),
        _const_spec(b_sp_rows.shape, 2),
    ]
    return pl.pallas_call(
        kern, grid=grid, in_specs=in_specs, out_specs=tuple(out_specs), out_shape=tuple(out_shape),
        scratch_shapes=[pltpu.VMEM((bt, ts + pad, d_c), F32)],
        compiler_params=_params("arbitrary", "arbitrary"),
    )(x, mod3, mod3, vec(g_pre), w_in, conv_w, vec(conv_b), vec(ln_c_g), vec(ln_c_b), vec(ln_v_g),
      vec(ln_v_b), hist, w_sp_cat, b_sp_rows)


def _out_ffn_kernel(y0_ref, y1_ref, x_ref, gm_ref, shf_ref, scf_ref, gf_ref, npost_ref, nfpre_ref, nfpost_ref,
                    wo_ref, wg_ref, wu_ref, wd_ref, o_ref, x1_ref, h_ref, acc_ref, *, bt, ts):
    f = pl.program_id(2)
    tm = bt * ts
    d = x_ref.shape[-1]
    k0 = y0_ref.shape[-1]

    @pl.when(f == 0)
    def _():
        y = _dot(y0_ref[...], wo_ref[0:k0, :]) + _dot(y1_ref[...], wo_ref[k0:, :])
        x1 = x_ref[...] + gm_ref[...] * _rms(y, npost_ref[...]).reshape(bt, ts, d)
        x1_ref[...] = x1
        h = _rms(x1, nfpre_ref[...]) * (1.0 + scf_ref[...]) + shf_ref[...]
        h_ref[...] = h.reshape(tm, d).astype(BF16)
        acc_ref[...] = jnp.zeros(acc_ref.shape, F32)

    h = h_ref[...]
    a = _silu(_dot(h, wg_ref[...])) * _dot(h, wu_ref[...])
    acc_ref[...] += _dot(a, wd_ref[...])

    @pl.when(f == pl.num_programs(2) - 1)
    def _():
        o_ref[...] = x1_ref[...] + gf_ref[...] * _rms(acc_ref[...], nfpost_ref[...]).reshape(bt, ts, d)


def _out_ffn(y0, y1, x, mod3, norm_post, norm_ffn_pre, norm_ffn_post, w_out, w_gate, w_up, w_down):
    b, s, d = x.shape
    bt, ts = _tiles(b, s)
    tm = bt * ts
    d_ff = w_gate.shape[1]
    n_f = 2
    tf = d_ff // n_f
    assert tf % LANES == 0
    grid = (b // bt, s // ts, n_f)
    row = lambda bb, ss, f: (bb * (s // ts) + ss, 0)
    vec = lambda a: a.reshape(1, -1)
    kern = functools.partial(_out_ffn_kernel, bt=bt, ts=ts)
    in_specs = [
        pl.BlockSpec((tm, y0.shape[1]), row),
        pl.BlockSpec((tm, y1.shape[1]), row),
        pl.BlockSpec((bt, ts, d), lambda bb, ss, f: (bb, ss, 0)),
        _mod_spec(bt, d, 2, 3), _mod_spec(bt, d, 3, 3), _mod_spec(bt, d, 4, 3), _mod_spec(bt, d, 5, 3),
        _const_spec((1, d), 3), _const_spec((1, d), 3), _const_spec((1, d), 3),
        _const_spec(w_out.shape, 3),
        pl.BlockSpec((d, tf), lambda bb, ss, f: (0, f)),
        pl.BlockSpec((d, tf), lambda bb, ss, f: (0, f)),
        pl.BlockSpec((tf, d), lambda bb, ss, f: (f, 0)),
    ]
    return pl.pallas_call(
        kern, grid=grid, in_specs=in_specs,
        out_specs=pl.BlockSpec((bt, ts, d), lambda bb, ss, f: (bb, ss, 0)),
        out_shape=jax.ShapeDtypeStruct((b, s, d), F32),
        scratch_shapes=[pltpu.VMEM((bt, ts, d), F32), pltpu.VMEM((tm, d), BF16), pltpu.VMEM((tm, d), F32)],
        compiler_params=_params("arbitrary", "arbitrary", "arbitrary"),
    )(y0, y1, x, mod3, mod3, mod3, mod3, vec(norm_post), vec(norm_ffn_pre), vec(norm_ffn_post),
      w_out, w_gate, w_up, w_down)


def _rope_tables(pos):
    inv_freq = ROPE_THETA ** (-jnp.arange(0, ROPE_DIM, 2, dtype=F32) / ROPE_DIM)
    ang = pos.astype(F32)[:, None] * inv_freq[None, :]
    cos, sin = jnp.cos(ang), jnp.sin(ang)
    cosq = jnp.tile(jnp.concatenate([cos, cos], axis=-1), (1, N_HEADS_B))
    sinq = jnp.tile(jnp.concatenate([-sin, sin], axis=-1), (1, N_HEADS_B))
    zeros = jnp.zeros((pos.shape[0], LANES - 2 * ROPE_DIM), F32)
    ropek = jnp.concatenate([cos, cos, -sin, sin, zeros], axis=-1)
    return cosq, sinq, ropek


def _prep_ab_weights(w_in, w_q_up, w_kv_up, d_a, q_rank, kv_rank):
    d = w_in.shape[0]
    o_kpe = 3 * d_a + q_rank + kv_rank
    half = ROPE_DIM // 2
    swapped = jnp.concatenate([w_in[:, o_kpe + half:o_kpe + ROPE_DIM], w_in[:, o_kpe:o_kpe + half]], axis=1)
    w_in_ext = jnp.concatenate(
        [w_in, swapped, jnp.zeros((d, LANES - 2 * ROPE_DIM), w_in.dtype)], axis=1).astype(BF16)
    wq = w_q_up.reshape(q_rank, N_HEADS_B, NOPE_DIM + ROPE_DIM)
    wq_rope = wq[:, :, NOPE_DIM:]
    wq_rope_sw = jnp.concatenate([wq_rope[:, :, half:], wq_rope[:, :, :half]], axis=2)
    w_q_perm = jnp.concatenate([
        wq[:, :, :NOPE_DIM].reshape(q_rank, -1),
        wq_rope.reshape(q_rank, -1),
        wq_rope_sw.reshape(q_rank, -1)], axis=1).astype(BF16)
    w_kv = w_kv_up.reshape(kv_rank, N_HEADS_B, NOPE_DIM + V_DIM)
    w_uk_t = jnp.transpose(w_kv[..., :NOPE_DIM], (1, 2, 0))
    w_uv = jnp.transpose(w_kv[..., NOPE_DIM:], (1, 0, 2))
    odd = (jnp.arange(N_HEADS_B) % 2 == 1)
    zk = jnp.zeros_like(w_uk_t)
    w_uk_pad = jnp.where(odd[:, None, None],
                         jnp.concatenate([zk, w_uk_t], axis=1),
                         jnp.concatenate([w_uk_t, zk], axis=1)).astype(BF16)
    zv = jnp.zeros_like(w_uv)
    w_uv_pad = jnp.where(odd[:, None, None],
                         jnp.concatenate([zv, w_uv], axis=2),
                         jnp.concatenate([w_uv, zv], axis=2)).astype(BF16)
    return w_in_ext, w_q_perm, w_uk_pad, w_uv_pad


def _prep_spatial(w_sp, b_sp, n):
    wm = jnp.tril(w_sp[:, :n, :n])
    reps = CHUNK // n
    eye = jnp.eye(reps, dtype=w_sp.dtype)
    big = jnp.einsum('ab,gts->gatbs', eye, wm).reshape(G_D, CHUNK, CHUNK)
    w_cat = jnp.concatenate([big[0::2], big[1::2]], axis=2).astype(BF16)
    b_rows = jnp.tile(b_sp[:, :n].T, (reps, 1))
    return w_cat, b_rows


def _trunk(x, mods, pos, hist_a, hist_c, attend, chunk_rows, with_v, ql_dtype, p):
    b, s, d = x.shape
    depth = p['w_ffn_gate'].shape[0]
    cosq, sinq, ropek = _rope_tables(pos)
    conv_a, ckvs, kpes, conv_c, vds = [], [], [], [], []
    for l in range(depth):
        mod3 = mods[l].reshape(b, 1, -1)
        i = l // 2
        if l % 2 == 0:
            w_in_ext, w_q_perm, w_uk_pad, w_uv_pad = p['ab'][i]
            ya, q_cat, ckv, kpe, k_cat, nconv = _ab_in(
                x, mod3, p['norm_mix_pre'][l], w_in_ext, p['conv_a_w'][i], hist_a[i], p['q_norm'][i],
                w_q_perm, p['kv_norm'][i], w_uk_pad, cosq, sinq, ropek, ql_dtype)
            y1 = attend(i, q_cat, ckv, kpe, k_cat, w_uv_pad)
            y0 = ya
            w_out = p['w_out_ab'][i]
            conv_a.append(nconv)
            ckvs.append(ckv.reshape(b, s, -1))
            kpes.append(kpe.reshape(b, s, -1))
        else:
            w_sp_cat, b_rows = _prep_spatial(p['w_spatial'][i], p['b_spatial'][i], chunk_rows)
            d_c = p['conv_c_w'].shape[-1]
            b_sp_rows = jnp.repeat(b_rows, d_c // G_D, axis=1)
            outs = _cd_in(x, mod3, p['norm_mix_pre'][l], p['w_in_cd'][i], p['conv_c_w'][i], p['conv_c_b'][i],
                          p['ln_c_g'][i], p['ln_c_b'][i], p['ln_v_g'][i], p['ln_v_b'][i], hist_c[i],
                          w_sp_cat, b_sp_rows, with_v)
            y0, y1, nconv = outs[:3]
            if with_v:
                vds.append(outs[3].reshape(b, s, -1))
            w_out = p['w_out_cd'][i]
            conv_c.append(nconv)
        x = _out_ffn(y0, y1, x, mod3, p['norm_mix_post'][l], p['norm_ffn_pre'][l], p['norm_ffn_post'][l],
                     w_out, p['w_ffn_gate'][l], p['w_ffn_up'][l], p['w_ffn_down'][l])
    return x, conv_a, ckvs, kpes, conv_c, vds


def kernel(x_prompt, x_sample, cache_ckv, cache_kpe, state_conv_a, state_conv_c, page_table, c_prompt, c_sample, w_mod, b_mod, norm_mix_pre, norm_mix_post, norm_ffn_pre, norm_ffn_post, w_in_ab, conv_a_w, q_norm, w_q_up, kv_norm, w_kv_up, w_out_ab, w_in_cd, conv_c_w, conv_c_b, ln_c_g, ln_c_b, ln_v_g, ln_v_b, w_spatial, b_spatial, w_out_cd, w_ffn_gate, w_ffn_up, w_ffn_down):
    b_p, s_p, d = x_prompt.shape
    b_s, s_s, _ = x_sample.shape
    n_ab, n_cd = w_in_ab.shape[0], w_in_cd.shape[0]
    d_a = conv_a_w.shape[-1]
    d_c = conv_c_w.shape[-1]
    q_rank, kv_rank = q_norm.shape[-1], kv_norm.shape[-1]

    p = {
        'norm_mix_pre': norm_mix_pre, 'norm_mix_post': norm_mix_post, 'norm_ffn_pre': norm_ffn_pre,
        'norm_ffn_post': norm_ffn_post, 'conv_a_w': conv_a_w, 'q_norm': q_norm, 'kv_norm': kv_norm,
        'w_out_ab': w_out_ab.astype(BF16), 'w_in_cd': w_in_cd.astype(BF16), 'conv_c_w': conv_c_w,
        'conv_c_b': conv_c_b, 'ln_c_g': ln_c_g, 'ln_c_b': ln_c_b, 'ln_v_g': ln_v_g, 'ln_v_b': ln_v_b,
        'w_spatial': w_spatial, 'b_spatial': b_spatial, 'w_out_cd': w_out_cd.astype(BF16),
        'w_ffn_gate': w_ffn_gate.astype(BF16), 'w_ffn_up': w_ffn_up.astype(BF16),
        'w_ffn_down': w_ffn_down.astype(BF16),
        'ab': [_prep_ab_weights(w_in_ab[i], w_q_up[i], w_kv_up[i], d_a, q_rank, kv_rank) for i in range(n_ab)],
    }

    mods = _modulation(jnp.concatenate([c_prompt, c_sample], axis=0), w_mod, b_mod)
    mods_p, mods_s = mods[:, :b_p], mods[:, b_p:]

    zeros_a = jnp.zeros((n_ab, b_p, state_conv_a.shape[2], d_a), F32)
    zeros_c = jnp.zeros((n_cd, b_p, state_conv_c.shape[2], d_c), F32)

    def attend_prompt(i, q_cat, ckv, kpe, k_cat, w_uv_pad):
        return _prompt_attention(q_cat, k_cat, w_uv_pad, b_p, s_p)

    y_p, sa_p, ckv_p, kpe_p, sc_p, _ = _trunk(
        x_prompt, mods_p, jnp.arange(s_p), zeros_a, zeros_c, attend_prompt, CHUNK, False, BF16, p)

    past_len = page_table.shape[1] * PAGE_SIZE

    cache_kpe_t = jnp.swapaxes(cache_kpe, 2, 3)

    def attend_sample(i, q_cat, ckv, kpe, k_cat, w_uv_pad):
        return _paged_attention(q_cat, ckv, kpe, cache_ckv, cache_kpe_t, page_table, i, w_uv_pad, s_s)

    y_s, sa_s, ckv_s, kpe_s, sc_s, vd_s = _trunk(
        x_sample, mods_s, past_len + jnp.arange(s_s), state_conv_a, state_conv_c, attend_sample, s_s,
        True, F32, p)

    st = jnp.stack
    return (y_p, y_s, st(sa_p), st(ckv_p), st(kpe_p), st(sc_p), st(sa_s), st(ckv_s), st(kpe_s), st(sc_s),
            st(vd_s))
```

```python
import functools

import jax
import jax.numpy as jnp
from jax import lax
from jax.experimental import pallas as pl
from jax.experimental.pallas import tpu as pltpu

N_HEADS_B = 8
NOPE_DIM = 64
ROPE_DIM = 32
V_DIM = 64
ROPE_THETA = 10000.0
G_D = 8
CHUNK = 128
PAGE_SIZE = 128
EPS = 1e-6
NEG_INF = -1e30
SCALE_B = (NOPE_DIM + ROPE_DIM) ** -0.5

LANES = 128
SUBLANES = 8
VMEM_LIMIT = 56 * 1024 * 1024

ROW_TILE = 512
ATTN_Q_TILE = 256
PAGES_PER_CHUNK = 32
PAGED_SLOTS = 4

F32 = jnp.float32
BF16 = jnp.bfloat16


def _rms(x, g):
    return x * lax.rsqrt(jnp.mean(x * x, axis=-1, keepdims=True) + EPS) * g


def _ln(x, g, b):
    xc = x - jnp.mean(x, axis=-1, keepdims=True)
    return xc * lax.rsqrt(jnp.mean(xc * xc, axis=-1, keepdims=True) + EPS) * g + b


def _silu(x):
    return x * jax.nn.sigmoid(x)


def _dot(a, b):
    return jnp.dot(a.astype(BF16), b.astype(BF16), preferred_element_type=F32)


def _dot_nt(a, b):
    return lax.dot_general(a.astype(BF16), b.astype(BF16), (((1,), (1,)), ((), ())),
                           preferred_element_type=F32)


def _params(*sem):
    return pltpu.CompilerParams(dimension_semantics=sem, vmem_limit_bytes=VMEM_LIMIT)


def _mod_kernel(c_ref, w_ref, b_ref, o_ref):
    o_ref[0] = _dot(_silu(c_ref[...]), w_ref[0]) + b_ref[0]


def _modulation(c_all, w_mod, b_mod):
    n_l, d, n = w_mod.shape
    bc = c_all.shape[0]
    tn = n // 4
    return pl.pallas_call(
        _mod_kernel,
        grid=(n_l, n // tn),
        in_specs=[pl.BlockSpec((bc, d), lambda l, j: (0, 0)),
                  pl.BlockSpec((1, d, tn), lambda l, j: (l, 0, j)),
                  pl.BlockSpec((1, 1, tn), lambda l, j: (l, 0, j))],
        out_specs=pl.BlockSpec((1, bc, tn), lambda l, j: (l, 0, j)),
        out_shape=jax.ShapeDtypeStruct((n_l, bc, n), F32),
        compiler_params=_params("arbitrary", "arbitrary"),
    )(c_all, w_mod, b_mod.reshape(n_l, 1, n))


def _mod_spec(bt, d, k, grid_rank):
    if grid_rank == 2:
        return pl.BlockSpec((bt, 1, d), lambda b, s: (b, 0, k))
    return pl.BlockSpec((bt, 1, d), lambda b, s, f: (b, 0, k))


def _const_spec(shape, grid_rank):
    zeros = (0,) * len(shape)
    if grid_rank == 2:
        return pl.BlockSpec(shape, lambda b, s: zeros)
    return pl.BlockSpec(shape, lambda b, s, f: zeros)


def _tiles(b, s):
    if s >= ROW_TILE:
        assert s % ROW_TILE == 0
        return 1, ROW_TILE
    assert ROW_TILE % s == 0 and s % SUBLANES == 0 and b % (ROW_TILE // s) == 0
    return ROW_TILE // s, s


def _ab_in_kernel(x_ref, sh_ref, sc_ref, g_ref, w_ref, cw_ref, hist_ref, qn_ref, wq_ref, kvn_ref,
                  wuk_ref, cosq_ref, sinq_ref, ropek_ref,
                  ya_ref, qcat_ref, ckv_ref, kpe_ref, kcat_ref, nconv_ref,
                  zs_ref, *, bt, ts, d_a, q_rank, kv_rank):
    tm = bt * ts
    hist_rows = nconv_ref.shape[1]
    x = x_ref[...]
    h = _rms(x, g_ref[...]) * (1.0 + sc_ref[...]) + sh_ref[...]
    proj = _dot(h.reshape(tm, h.shape[-1]), w_ref[...])
    o_cq = 3 * d_a
    o_ckv = o_cq + q_rank
    o_kpe = o_ckv + kv_rank
    gate_out = proj[:, 0:d_a].reshape(bt, ts, d_a)
    z = (proj[:, d_a:2 * d_a] * proj[:, 2 * d_a:3 * d_a]).reshape(bt, ts, d_a)

    base = SUBLANES - hist_rows

    @pl.when(pl.program_id(1) == 0)
    def _():
        zs_ref[:, base:SUBLANES, :] = hist_ref[...]

    zs_ref[:, SUBLANES:SUBLANES + ts, :] = z
    cw = cw_ref[...]
    conv = z * cw[hist_rows:hist_rows + 1, :]
    for k in range(hist_rows):
        conv = conv + zs_ref[:, base + k:base + k + ts, :] * cw[k:k + 1, :]
    ya_ref[...] = (gate_out * conv).reshape(tm, d_a).astype(ya_ref.dtype)
    tail = zs_ref[:, ts + base:ts + SUBLANES, :]
    nconv_ref[...] = tail
    zs_ref[:, base:SUBLANES, :] = tail

    cqn = _rms(proj[:, o_cq:o_ckv], qn_ref[...])
    q = _dot(cqn, wq_ref[...]) * SCALE_B
    n_nope = N_HEADS_B * NOPE_DIM
    n_rope = N_HEADS_B * ROPE_DIM
    qr = q[:, n_nope:n_nope + n_rope].reshape(bt, ts, n_rope)
    qs = q[:, n_nope + n_rope:n_nope + 2 * n_rope].reshape(bt, ts, n_rope)
    qpe = (qr * cosq_ref[...][None] + qs * sinq_ref[...][None]).reshape(tm, n_rope)
    for hh in range(N_HEADS_B):
        pair = q[:, LANES * (hh // 2):LANES * (hh // 2 + 1)]
        q_lat = _dot(pair, wuk_ref[hh])
        qcat_ref[hh] = jnp.concatenate(
            [q_lat, qpe[:, hh * ROPE_DIM:(hh + 1) * ROPE_DIM]], axis=-1).astype(qcat_ref.dtype)

    ckvn = _rms(proj[:, o_ckv:o_kpe], kvn_ref[...])
    ckv_ref[...] = ckvn
    kt = proj[:, o_kpe:o_kpe + LANES].reshape(bt, ts, LANES) * ropek_ref[...][None]
    kpe = (kt[:, :, 0:ROPE_DIM] + kt[:, :, ROPE_DIM:2 * ROPE_DIM]).reshape(tm, ROPE_DIM)
    kpe_ref[...] = kpe
    kcat_ref[...] = jnp.concatenate([ckvn, kpe], axis=-1).astype(BF16)


def _ab_in(x, mod3, g_pre, w_in_ext, conv_w, hist, q_norm, w_q_perm, kv_norm, w_uk_pad,
           cosq, sinq, ropek, ql_dtype):
    b, s, d = x.shape
    m = b * s
    bt, ts = _tiles(b, s)
    tm = bt * ts
    d_a = conv_w.shape[1]
    q_rank = q_norm.shape[-1]
    kv_rank = kv_norm.shape[-1]
    hist_rows = hist.shape[1]
    n_ext = w_in_ext.shape[1]
    grid = (b // bt, s // ts)
    row = lambda bb, ss: (bb * (s // ts) + ss, 0)
    kern = functools.partial(_ab_in_kernel, bt=bt, ts=ts, d_a=d_a, q_rank=q_rank, kv_rank=kv_rank)
    d_qk = kv_rank + ROPE_DIM
    out_shape = (
        jax.ShapeDtypeStruct((m, d_a), BF16),
        jax.ShapeDtypeStruct((N_HEADS_B, m, d_qk), ql_dtype),
        jax.ShapeDtypeStruct((m, kv_rank), F32),
        jax.ShapeDtypeStruct((m, ROPE_DIM), F32),
        jax.ShapeDtypeStruct((m, d_qk), BF16),
        jax.ShapeDtypeStruct((b, hist_rows, d_a), F32),
    )
    out_specs = (
        pl.BlockSpec((tm, d_a), row),
        pl.BlockSpec((N_HEADS_B, tm, d_qk), lambda bb, ss: (0, bb * (s // ts) + ss, 0)),
        pl.BlockSpec((tm, kv_rank), row),
        pl.BlockSpec((tm, ROPE_DIM), row),
        pl.BlockSpec((tm, d_qk), row),
        pl.BlockSpec((bt, hist_rows, d_a), lambda bb, ss: (bb, 0, 0)),
    )
    in_specs = [
        pl.BlockSpec((bt, ts, d), lambda bb, ss: (bb, ss, 0)),
        _mod_spec(bt, d, 0, 2), _mod_spec(bt, d, 1, 2),
        _const_spec((1, d), 2),
        _const_spec((d, n_ext), 2),
        _const_spec(conv_w.shape, 2),
        pl.BlockSpec((bt, hist_rows, d_a), lambda bb, ss: (bb, 0, 0)),
        _const_spec((1, q_rank), 2),
        _const_spec(w_q_perm.shape, 2),
        _const_spec((1, kv_rank), 2),
        _const_spec(w_uk_pad.shape, 2),
        pl.BlockSpec((ts, cosq.shape[1]), lambda bb, ss: (ss, 0)),
        pl.BlockSpec((ts, sinq.shape[1]), lambda bb, ss: (ss, 0)),
        pl.BlockSpec((ts, LANES), lambda bb, ss: (ss, 0)),
    ]
    return pl.pallas_call(
        kern, grid=grid, in_specs=in_specs, out_specs=out_specs, out_shape=out_shape,
        scratch_shapes=[pltpu.VMEM((bt, ts + SUBLANES, d_a), F32)],
        compiler_params=_params("arbitrary", "arbitrary"),
    )(x, mod3, mod3, g_pre.reshape(1, d), w_in_ext, conv_w, hist, q_norm.reshape(1, q_rank),
      w_q_perm, kv_norm.reshape(1, kv_rank), w_uk_pad, cosq, sinq, ropek)


def _fold_lanes(x, op):
    out = x[:, 0:LANES]
    for t in range(1, x.shape[1] // LANES):
        out = op(out, x[:, t * LANES:(t + 1) * LANES])
    return out


def _value_up(o, wuv_ref, rows):
    parts = []
    for j in range(N_HEADS_B // 2):
        o0 = o[(2 * j) * rows:(2 * j + 1) * rows]
        o1 = o[(2 * j + 1) * rows:(2 * j + 2) * rows]
        parts.append(_dot(o0, wuv_ref[2 * j]) + _dot(o1, wuv_ref[2 * j + 1]))
    return jnp.concatenate(parts, axis=-1)


def _prompt_attn_kernel(q_ref, k_ref, wuv_ref, yb_ref, s_ref, stat_ref, lsum_ref, acc_ref, *, tq, r):
    i = pl.program_id(1)
    rows = N_HEADS_B * tq
    q = q_ref[...].reshape(rows, q_ref.shape[-1])

    def key_block(j):
        return k_ref[pl.ds(pl.multiple_of(j * tq, tq), tq), :]

    def scores_pass(j, mp):
        s = _dot_nt(q, key_block(j))
        s_ref[j] = s
        return jnp.maximum(mp, _fold_lanes(s, jnp.maximum))

    stat_ref[...] = jnp.full(stat_ref.shape, NEG_INF, F32)

    def scores_body(j, carry):
        stat_ref[...] = scores_pass(j, stat_ref[...])
        return carry

    lax.fori_loop(0, i, scores_body, 0)
    s = _dot_nt(q, key_block(i))
    qpos = lax.broadcasted_iota(jnp.int32, s.shape, 0) & (tq - 1)
    kpos = lax.broadcasted_iota(jnp.int32, s.shape, 1)
    s = jnp.where(kpos <= qpos, s, NEG_INF)
    s_ref[i] = s
    mp = jnp.maximum(stat_ref[...], _fold_lanes(s, jnp.maximum))
    m = jnp.max(mp, axis=-1, keepdims=True)
    stat_ref[...] = jnp.broadcast_to(m, stat_ref.shape)
    acc_ref[...] = jnp.zeros(acc_ref.shape, F32)

    lsum_ref[...] = jnp.zeros(lsum_ref.shape, F32)

    def value_body(j, carry):
        mrow = stat_ref[...]
        p = jnp.exp(s_ref[j] - jnp.concatenate([mrow] * (tq // LANES), axis=-1))
        acc_ref[...] += jnp.dot(p.astype(BF16), key_block(j)[:, 0:r], preferred_element_type=F32)
        lsum_ref[...] += _fold_lanes(p, jnp.add)
        return carry

    lax.fori_loop(0, i + 1, value_body, 0)
    o = acc_ref[...] / jnp.sum(lsum_ref[...], axis=-1, keepdims=True)
    yb_ref[...] = _value_up(o, wuv_ref, tq).astype(yb_ref.dtype)


def _prompt_attention(q_cat, k_cat, w_uv_pad, b, s):
    n_h, m, d_qk = q_cat.shape
    r = d_qk - ROPE_DIM
    tq = ATTN_Q_TILE
    nq = s // tq
    rows = n_h * tq
    kern = functools.partial(_prompt_attn_kernel, tq=tq, r=r)
    return pl.pallas_call(
        kern, grid=(b, nq),
        in_specs=[pl.BlockSpec((n_h, tq, d_qk), lambda bb, i: (0, bb * nq + i, 0)),
                  pl.BlockSpec((s, d_qk), lambda bb, i: (bb, 0)),
                  _const_spec(w_uv_pad.shape, 2)],
        out_specs=pl.BlockSpec((tq, n_h * V_DIM), lambda bb, i: (bb * nq + i, 0)),
        out_shape=jax.ShapeDtypeStruct((m, n_h * V_DIM), BF16),
        scratch_shapes=[pltpu.VMEM((nq, rows, tq), F32), pltpu.VMEM((rows, LANES), F32),
                        pltpu.VMEM((rows, LANES), F32), pltpu.VMEM((rows, r), F32)],
        compiler_params=_params("arbitrary", "arbitrary"),
    )(q_cat, k_cat, w_uv_pad)


def _paged_attn_kernel(pt_ref, q_ref, cnew_ref, knew_ref, wuv_ref, ckv_hbm, kpet_hbm, yb_ref,
                       raw_k, raw_p, kb_ref, s_ref, sems, *, layer, n_chunks, n_slots, pages_chunk, page,
                       s_new, r):
    b = pl.program_id(0)
    keys_chunk = pages_chunk * page
    rows = N_HEADS_B * s_new
    qc = q_ref[...].reshape(rows, q_ref.shape[-1])
    q_lat = qc[:, 0:r].astype(BF16)
    q_pe = qc[:, r:].astype(BF16)

    def chunk_copies(seq, c, slot):
        copies = []
        for i in range(pages_chunk):
            pg = pt_ref[seq, c * pages_chunk + i]
            copies.append(pltpu.make_async_copy(
                ckv_hbm.at[layer, pg], raw_k.at[slot, pl.ds(i * page, page)], sems.at[0, slot]))
            copies.append(pltpu.make_async_copy(
                kpet_hbm.at[layer, pg], raw_p.at[slot, :, pl.ds(i * page, page)], sems.at[1, slot]))
        return copies

    ahead = n_slots - 1

    @pl.when(b == 0)
    def _():
        for c in range(ahead):
            for cp in chunk_copies(b, c, c % n_slots):
                cp.start()

    mp = jnp.full((rows, LANES), NEG_INF, F32)
    for c in range(n_chunks):
        slot = c % n_slots
        for cp in chunk_copies(b, c, slot):
            cp.wait()
        nxt = c + ahead
        if nxt < n_chunks:
            for cp in chunk_copies(b, nxt, nxt % n_slots):
                cp.start()
        else:
            @pl.when(b + 1 < pl.num_programs(0))
            def _():
                for cp in chunk_copies(b + 1, nxt - n_chunks, nxt % n_slots):
                    cp.start()
        k = raw_k[slot].astype(BF16)
        kb_ref[c * keys_chunk:(c + 1) * keys_chunk, :] = k
        s = _dot_nt(q_lat, k) + jnp.dot(q_pe, raw_p[slot].astype(BF16), preferred_element_type=F32)
        s_ref[:, c * keys_chunk:(c + 1) * keys_chunk] = s
        mp = jnp.maximum(mp, _fold_lanes(s, jnp.maximum))

    k_new = cnew_ref[...].astype(BF16)
    s_n = _dot_nt(q_lat, k_new) + _dot_nt(q_pe, knew_ref[...])
    qpos = lax.broadcasted_iota(jnp.int32, s_n.shape, 0) & (s_new - 1)
    kpos = lax.broadcasted_iota(jnp.int32, s_n.shape, 1)
    s_n = jnp.where(kpos <= qpos, s_n, NEG_INF)
    m = jnp.maximum(jnp.max(mp, axis=-1, keepdims=True), jnp.max(s_n, axis=-1, keepdims=True))

    p_n = jnp.exp(s_n - m)
    acc = jnp.dot(p_n.astype(BF16), k_new, preferred_element_type=F32)
    lp = jnp.zeros((rows, LANES), F32)
    for c in range(n_chunks):
        p = jnp.exp(s_ref[:, c * keys_chunk:(c + 1) * keys_chunk] - m)
        lp = lp + _fold_lanes(p, jnp.add)
        acc = acc + jnp.dot(p.astype(BF16), kb_ref[c * keys_chunk:(c + 1) * keys_chunk, :],
                            preferred_element_type=F32)
    l = jnp.sum(lp, axis=-1, keepdims=True) + jnp.sum(p_n, axis=-1, keepdims=True)
    yb_ref[...] = _value_up(acc / l, wuv_ref, s_new).astype(yb_ref.dtype)


def _paged_attention(q_cat, ckv_new, kpe_new, cache_ckv, cache_kpe_t, page_table, layer, w_uv_pad, s_new):
    n_h, m, d_qk = q_cat.shape
    r = d_qk - ROPE_DIM
    b, n_pages = page_table.shape
    page = cache_ckv.shape[2]
    pages_chunk = min(PAGES_PER_CHUNK, n_pages)
    n_chunks = n_pages // pages_chunk
    n_slots = PAGED_SLOTS
    assert s_new & (s_new - 1) == 0 and n_pages % pages_chunk == 0 and n_chunks % n_slots == 0
    keys_chunk = pages_chunk * page
    rows = n_h * s_new
    kern = functools.partial(_paged_attn_kernel, layer=layer, n_chunks=n_chunks, n_slots=n_slots,
                             pages_chunk=pages_chunk, page=page, s_new=s_new, r=r)
    grid_spec = pltpu.PrefetchScalarGridSpec(
        num_scalar_prefetch=1, grid=(b,),
        in_specs=[pl.BlockSpec((n_h, s_new, d_qk), lambda bb, pt: (0, bb, 0)),
                  pl.BlockSpec((s_new, r), lambda bb, pt: (bb, 0)),
                  pl.BlockSpec((s_new, ROPE_DIM), lambda bb, pt: (bb, 0)),
                  pl.BlockSpec(w_uv_pad.shape, lambda bb, pt: (0, 0, 0)),
                  pl.BlockSpec(memory_space=pl.ANY),
                  pl.BlockSpec(memory_space=pl.ANY)],
        out_specs=pl.BlockSpec((s_new, n_h * V_DIM), lambda bb, pt: (bb, 0)),
        scratch_shapes=[pltpu.VMEM((n_slots, keys_chunk, r), F32),
                        pltpu.VMEM((n_slots, ROPE_DIM, keys_chunk), F32),
                        pltpu.VMEM((n_pages * page, r), BF16),
                        pltpu.VMEM((rows, n_pages * page), F32),
                        pltpu.SemaphoreType.DMA((2, n_slots))])
    return pl.pallas_call(
        kern, grid_spec=grid_spec,
        out_shape=jax.ShapeDtypeStruct((m, n_h * V_DIM), F32),
        compiler_params=_params("arbitrary"),
    )(page_table, q_cat, ckv_new, kpe_new, w_uv_pad, cache_ckv, cache_kpe_t)


def _cd_in_kernel(x_ref, sh_ref, sc_ref, g_ref, w_ref, cw_ref, cb_ref, lcg_ref, lcb_ref, lvg_ref, lvb_ref,
                  hist_ref, wsp_ref, bsp_ref, *rest, bt, ts, d_c, with_v):
    if with_v:
        yc_ref, yd_ref, nconv_ref, vn_ref, hs_ref = rest
    else:
        yc_ref, yd_ref, nconv_ref, hs_ref = rest
    tm = bt * ts
    hist_rows = nconv_ref.shape[1]
    n_taps = hist_rows + 1
    pad = hs_ref.shape[1] - ts
    base = pad - hist_rows
    x = x_ref[...]
    h = _rms(x, g_ref[...]) * (1.0 + sc_ref[...]) + sh_ref[...]
    proj = _dot(h.reshape(tm, h.shape[-1]), w_ref[...])

    glu = (proj[:, 0:d_c] * jax.nn.sigmoid(proj[:, d_c:2 * d_c])).reshape(bt, ts, d_c)

    @pl.when(pl.program_id(1) == 0)
    def _():
        if base:
            hs_ref[:, 0:base, :] = jnp.zeros((bt, base, d_c), F32)
        hs_ref[:, base:pad, :] = hist_ref[...]

    hs_ref[:, pad:pad + ts, :] = glu
    cw = cw_ref[...]
    conv = glu * cw[n_taps - 1:n_taps, :] + cb_ref[...]
    for rem in range(SUBLANES):
        part = None
        for row0 in range(0, pad, SUBLANES):
            k = row0 + rem - base
            if 0 <= k < n_taps - 1:
                term = hs_ref[:, row0:row0 + ts + SUBLANES, :] * cw[k:k + 1, :]
                part = term if part is None else part + term
        if part is not None:
            conv = conv + part[:, rem:rem + ts, :]
    yc = _silu(_ln(conv, lcg_ref[...], lcb_ref[...]))
    yc_ref[...] = yc.reshape(tm, d_c).astype(yc_ref.dtype)
    tail = hs_ref[:, ts + base:ts + pad, :]
    nconv_ref[...] = tail
    hs_ref[:, base:pad, :] = tail

    u = proj[:, 2 * d_c:3 * d_c]
    vn = _ln(proj[:, 3 * d_c:4 * d_c], lvg_ref[...], lvb_ref[...])
    if with_v:
        vn_ref[...] = vn
    low = lax.broadcasted_iota(jnp.int32, (CHUNK, LANES), 1) < (LANES // 2)
    bsp = bsp_ref[...]
    for c in range(tm // CHUNK):
        parts = []
        for j in range(d_c // LANES):
            vj = vn[c * CHUNK:(c + 1) * CHUNK, j * LANES:(j + 1) * LANES]
            rhs = jnp.concatenate([jnp.where(low, vj, 0.0), jnp.where(low, 0.0, vj)], axis=0)
            parts.append(_dot(wsp_ref[j], rhs))
        sv = jnp.concatenate(parts, axis=-1) + bsp
        yd_ref[c * CHUNK:(c + 1) * CHUNK, :] = (u[c * CHUNK:(c + 1) * CHUNK] * sv).astype(yd_ref.dtype)


def _cd_in(x, mod3, g_pre, w_in, conv_w, conv_b, ln_c_g, ln_c_b, ln_v_g, ln_v_b, hist, w_sp_cat, b_sp_rows,
           with_v):
    b, s, d = x.shape
    m = b * s
    bt, ts = _tiles(b, s)
    tm = bt * ts
    d_c = conv_w.shape[1]
    hist_rows = hist.shape[1]
    pad = -(-hist_rows // SUBLANES) * SUBLANES
    grid = (b // bt, s // ts)
    row = lambda bb, ss: (bb * (s // ts) + ss, 0)
    vec = lambda a: a.reshape(1, -1)
    kern = functools.partial(_cd_in_kernel, bt=bt, ts=ts, d_c=d_c, with_v=with_v)
    out_shape = [jax.ShapeDtypeStruct((m, d_c), BF16), jax.ShapeDtypeStruct((m, d_c), BF16),
                 jax.ShapeDtypeStruct((b, hist_rows, d_c), F32)]
    out_specs = [pl.BlockSpec((tm, d_c), row), pl.BlockSpec((tm, d_c), row),
                 pl.BlockSpec((bt, hist_rows, d_c), lambda bb, ss: (bb, 0, 0))]
    if with_v:
        out_shape.append(jax.ShapeDtypeStruct((m, d_c), F32))
        out_specs.append(pl.BlockSpec((tm, d_c), row))
    in_specs = [
        pl.BlockSpec((bt, ts, d), lambda bb, ss: (bb, ss, 0)),
        _mod_spec(bt, d, 0, 2), _mod_spec(bt, d, 1, 2),
        _const_spec((1, d), 2),
        _const_spec(w_in.shape, 2),
        _const_spec(conv_w.shape, 2),
        _const_spec((1, d_c), 2), _const_spec((1, d_c), 2), _const_spec((1, d_c), 2),
        _const_spec((1, d_c), 2), _const_spec((1, d_c), 2),
        pl.BlockSpec((bt, hist_rows, d_c), lambda bb, ss: (bb, 0, 0)),
        _const_spec(w_sp_cat.shape, 2),
        _const_spec(b_sp_rows.shape, 2),
    ]
    return pl.pallas_call(
        kern, grid=grid, in_specs=in_specs, out_specs=tuple(out_specs), out_shape=tuple(out_shape),
        scratch_shapes=[pltpu.VMEM((bt, ts + pad, d_c), F32)],
        compiler_params=_params("arbitrary", "arbitrary"),
    )(x, mod3, mod3, vec(g_pre), w_in, conv_w, vec(conv_b), vec(ln_c_g), vec(ln_c_b), vec(ln_v_g),
      vec(ln_v_b), hist, w_sp_cat, b_sp_rows)


def _out_ffn_kernel(y0_ref, y1_ref, x_ref, gm_ref, shf_ref, scf_ref, gf_ref, npost_ref, nfpre_ref, nfpost_ref,
                    wo_ref, wg_ref, wu_ref, wd_ref, o_ref, x1_ref, h_ref, acc_ref, *, bt, ts):
    f = pl.program_id(2)
    tm = bt * ts
    d = x_ref.shape[-1]
    k0 = y0_ref.shape[-1]

    @pl.when(f == 0)
    def _():
        y = _dot(y0_ref[...], wo_ref[0:k0, :]) + _dot(y1_ref[...], wo_ref[k0:, :])
        x1 = x_ref[...] + gm_ref[...] * _rms(y, npost_ref[...]).reshape(bt, ts, d)
        x1_ref[...] = x1
        h = _rms(x1, nfpre_ref[...]) * (1.0 + scf_ref[...]) + shf_ref[...]
        h_ref[...] = h.reshape(tm, d).astype(BF16)
        acc_ref[...] = jnp.zeros(acc_ref.shape, F32)

    h = h_ref[...]
    a = _silu(_dot(h, wg_ref[...])) * _dot(h, wu_ref[...])
    acc_ref[...] += _dot(a, wd_ref[...])

    @pl.when(f == pl.num_programs(2) - 1)
    def _():
        o_ref[...] = x1_ref[...] + gf_ref[...] * _rms(acc_ref[...], nfpost_ref[...]).reshape(bt, ts, d)


def _out_ffn(y0, y1, x, mod3, norm_post, norm_ffn_pre, norm_ffn_post, w_out, w_gate, w_up, w_down, layer):
    b, s, d = x.shape
    bt, ts = _tiles(b, s)
    tm = bt * ts
    d_ff = w_gate.shape[2]
    n_f = 2
    tf = d_ff // n_f
    assert tf % LANES == 0
    grid = (b // bt, s // ts, n_f)
    row = lambda bb, ss, f: (bb * (s // ts) + ss, 0)
    vec = lambda a: a.reshape(1, -1)
    kern = functools.partial(_out_ffn_kernel, bt=bt, ts=ts)
    in_specs = [
        pl.BlockSpec((tm, y0.shape[1]), row),
        pl.BlockSpec((tm, y1.shape[1]), row),
        pl.BlockSpec((bt, ts, d), lambda bb, ss, f: (bb, ss, 0)),
        _mod_spec(bt, d, 2, 3), _mod_spec(bt, d, 3, 3), _mod_spec(bt, d, 4, 3), _mod_spec(bt, d, 5, 3),
        _const_spec((1, d), 3), _const_spec((1, d), 3), _const_spec((1, d), 3),
        _const_spec(w_out.shape, 3),
        pl.BlockSpec((None, d, tf), lambda bb, ss, f: (layer, 0, f)),
        pl.BlockSpec((None, d, tf), lambda bb, ss, f: (layer, 0, f)),
        pl.BlockSpec((None, tf, d), lambda bb, ss, f: (layer, f, 0)),
    ]
    return pl.pallas_call(
        kern, grid=grid, in_specs=in_specs,
        out_specs=pl.BlockSpec((bt, ts, d), lambda bb, ss, f: (bb, ss, 0)),
        out_shape=jax.ShapeDtypeStruct((b, s, d), F32),
        scratch_shapes=[pltpu.VMEM((bt, ts, d), F32), pltpu.VMEM((tm, d), BF16), pltpu.VMEM((tm, d), F32)],
        compiler_params=_params("arbitrary", "arbitrary", "arbitrary"),
    )(y0, y1, x, mod3, mod3, mod3, mod3, vec(norm_post), vec(norm_ffn_pre), vec(norm_ffn_post),
      w_out, w_gate, w_up, w_down)


def _rope_tables(pos):
    inv_freq = ROPE_THETA ** (-jnp.arange(0, ROPE_DIM, 2, dtype=F32) / ROPE_DIM)
    ang = pos.astype(F32)[:, None] * inv_freq[None, :]
    cos, sin = jnp.cos(ang), jnp.sin(ang)
    cosq = jnp.tile(jnp.concatenate([cos, cos], axis=-1), (1, N_HEADS_B))
    sinq = jnp.tile(jnp.concatenate([-sin, sin], axis=-1), (1, N_HEADS_B))
    zeros = jnp.zeros((pos.shape[0], LANES - 2 * ROPE_DIM), F32)
    ropek = jnp.concatenate([cos, cos, -sin, sin, zeros], axis=-1)
    return cosq, sinq, ropek


def _prep_ab_weights(w_in, w_q_up, w_kv_up, d_a, q_rank, kv_rank):
    d = w_in.shape[0]
    o_kpe = 3 * d_a + q_rank + kv_rank
    half = ROPE_DIM // 2
    swapped = jnp.concatenate([w_in[:, o_kpe + half:o_kpe + ROPE_DIM], w_in[:, o_kpe:o_kpe + half]], axis=1)
    w_in_ext = jnp.concatenate(
        [w_in, swapped, jnp.zeros((d, LANES - 2 * ROPE_DIM), w_in.dtype)], axis=1).astype(BF16)
    wq = w_q_up.reshape(q_rank, N_HEADS_B, NOPE_DIM + ROPE_DIM)
    wq_rope = wq[:, :, NOPE_DIM:]
    wq_rope_sw = jnp.concatenate([wq_rope[:, :, half:], wq_rope[:, :, :half]], axis=2)
    w_q_perm = jnp.concatenate([
        wq[:, :, :NOPE_DIM].reshape(q_rank, -1),
        wq_rope.reshape(q_rank, -1),
        wq_rope_sw.reshape(q_rank, -1)], axis=1).astype(BF16)
    w_kv = w_kv_up.reshape(kv_rank, N_HEADS_B, NOPE_DIM + V_DIM)
    w_uk_t = jnp.transpose(w_kv[..., :NOPE_DIM], (1, 2, 0))
    w_uv = jnp.transpose(w_kv[..., NOPE_DIM:], (1, 0, 2))
    odd = (jnp.arange(N_HEADS_B) % 2 == 1)
    zk = jnp.zeros_like(w_uk_t)
    w_uk_pad = jnp.where(odd[:, None, None],
                         jnp.concatenate([zk, w_uk_t], axis=1),
                         jnp.concatenate([w_uk_t, zk], axis=1)).astype(BF16)
    zv = jnp.zeros_like(w_uv)
    w_uv_pad = jnp.where(odd[:, None, None],
                         jnp.concatenate([zv, w_uv], axis=2),
                         jnp.concatenate([w_uv, zv], axis=2)).astype(BF16)
    return w_in_ext, w_q_perm, w_uk_pad, w_uv_pad


def _prep_spatial(w_sp, b_sp, n):
    wm = jnp.tril(w_sp[:, :n, :n])
    reps = CHUNK // n
    eye = jnp.eye(reps, dtype=w_sp.dtype)
    big = jnp.einsum('ab,gts->gatbs', eye, wm).reshape(G_D, CHUNK, CHUNK)
    w_cat = jnp.concatenate([big[0::2], big[1::2]], axis=2).astype(BF16)
    b_rows = jnp.tile(b_sp[:, :n].T, (reps, 1))
    return w_cat, b_rows


def _trunk(x, mods, pos, hist_a, hist_c, attend, chunk_rows, with_v, ql_dtype, p):
    b, s, d = x.shape
    depth = p['w_ffn_gate'].shape[0]
    cosq, sinq, ropek = _rope_tables(pos)
    conv_a, ckvs, kpes, conv_c, vds = [], [], [], [], []
    for l in range(depth):
        mod3 = mods[l].reshape(b, 1, -1)
        i = l // 2
        if l % 2 == 0:
            w_in_ext, w_q_perm, w_uk_pad, w_uv_pad = p['ab'][i]
            ya, q_cat, ckv, kpe, k_cat, nconv = _ab_in(
                x, mod3, p['norm_mix_pre'][l], w_in_ext, p['conv_a_w'][i], hist_a[i], p['q_norm'][i],
                w_q_perm, p['kv_norm'][i], w_uk_pad, cosq, sinq, ropek, ql_dtype)
            y1 = attend(i, q_cat, ckv, kpe, k_cat, w_uv_pad)
            y0 = ya
            w_out = p['w_out_ab'][i]
            conv_a.append(nconv)
            ckvs.append(ckv.reshape(b, s, -1))
            kpes.append(kpe.reshape(b, s, -1))
        else:
            w_sp_cat, b_rows = _prep_spatial(p['w_spatial'][i], p['b_spatial'][i], chunk_rows)
            d_c = p['conv_c_w'].shape[-1]
            b_sp_rows = jnp.repeat(b_rows, d_c // G_D, axis=1)
            outs = _cd_in(x, mod3, p['norm_mix_pre'][l], p['w_in_cd'][i], p['conv_c_w'][i], p['conv_c_b'][i],
                          p['ln_c_g'][i], p['ln_c_b'][i], p['ln_v_g'][i], p['ln_v_b'][i], hist_c[i],
                          w_sp_cat, b_sp_rows, with_v)
            y0, y1, nconv = outs[:3]
            if with_v:
                vds.append(outs[3].reshape(b, s, -1))
            w_out = p['w_out_cd'][i]
            conv_c.append(nconv)
        x = _out_ffn(y0, y1, x, mod3, p['norm_mix_post'][l], p['norm_ffn_pre'][l], p['norm_ffn_post'][l],
                     w_out, p['w_ffn_gate'], p['w_ffn_up'], p['w_ffn_down'], l)
    return x, conv_a, ckvs, kpes, conv_c, vds


def kernel(x_prompt, x_sample, cache_ckv, cache_kpe, state_conv_a, state_conv_c, page_table, c_prompt, c_sample, w_mod, b_mod, norm_mix_pre, norm_mix_post, norm_ffn_pre, norm_ffn_post, w_in_ab, conv_a_w, q_norm, w_q_up, kv_norm, w_kv_up, w_out_ab, w_in_cd, conv_c_w, conv_c_b, ln_c_g, ln_c_b, ln_v_g, ln_v_b, w_spatial, b_spatial, w_out_cd, w_ffn_gate, w_ffn_up, w_ffn_down):
    b_p, s_p, d = x_prompt.shape
    b_s, s_s, _ = x_sample.shape
    n_ab, n_cd = w_in_ab.shape[0], w_in_cd.shape[0]
    d_a = conv_a_w.shape[-1]
    d_c = conv_c_w.shape[-1]
    q_rank, kv_rank = q_norm.shape[-1], kv_norm.shape[-1]

    p = {
        'norm_mix_pre': norm_mix_pre, 'norm_mix_post': norm_mix_post, 'norm_ffn_pre': norm_ffn_pre,
        'norm_ffn_post': norm_ffn_post, 'conv_a_w': conv_a_w, 'q_norm': q_norm, 'kv_norm': kv_norm,
        'w_out_ab': w_out_ab.astype(BF16), 'w_in_cd': w_in_cd.astype(BF16), 'conv_c_w': conv_c_w,
        'conv_c_b': conv_c_b, 'ln_c_g': ln_c_g, 'ln_c_b': ln_c_b, 'ln_v_g': ln_v_g, 'ln_v_b': ln_v_b,
        'w_spatial': w_spatial, 'b_spatial': b_spatial, 'w_out_cd': w_out_cd.astype(BF16),
        'w_ffn_gate': w_ffn_gate.astype(BF16), 'w_ffn_up': w_ffn_up.astype(BF16),
        'w_ffn_down': w_ffn_down.astype(BF16),
        'ab': [_prep_ab_weights(w_in_ab[i], w_q_up[i], w_kv_up[i], d_a, q_rank, kv_rank) for i in range(n_ab)],
    }

    mods = _modulation(jnp.concatenate([c_prompt, c_sample], axis=0), w_mod, b_mod)
    mods_p, mods_s = mods[:, :b_p], mods[:, b_p:]

    zeros_a = jnp.zeros((n_ab, b_p, state_conv_a.shape[2], d_a), F32)
    zeros_c = jnp.zeros((n_cd, b_p, state_conv_c.shape[2], d_c), F32)

    def attend_prompt(i, q_cat, ckv, kpe, k_cat, w_uv_pad):
        return _prompt_attention(q_cat, k_cat, w_uv_pad, b_p, s_p)

    y_p, sa_p, ckv_p, kpe_p, sc_p, _ = _trunk(
        x_prompt, mods_p, jnp.arange(s_p), zeros_a, zeros_c, attend_prompt, CHUNK, False, BF16, p)

    past_len = page_table.shape[1] * PAGE_SIZE

    cache_kpe_t = jnp.swapaxes(cache_kpe, 2, 3)

    def attend_sample(i, q_cat, ckv, kpe, k_cat, w_uv_pad):
        return _paged_attention(q_cat, ckv, kpe, cache_ckv, cache_kpe_t, page_table, i, w_uv_pad, s_s)

    y_s, sa_s, ckv_s, kpe_s, sc_s, vd_s = _trunk(
        x_sample, mods_s, past_len + jnp.arange(s_s), state_conv_a, state_conv_c, attend_sample, s_s,
        True, F32, p)

    st = jnp.stack
    return (y_p, y_s, st(sa_p), st(ckv_p), st(kpe_p), st(sc_p), st(sa_s), st(ckv_s), st(kpe_s), st(sc_s),
            st(vd_s))
```

```python
import functools

import jax
import jax.numpy as jnp
from jax import lax
from jax.experimental import pallas as pl
from jax.experimental.pallas import tpu as pltpu

N_HEADS_B = 8
NOPE_DIM = 64
ROPE_DIM = 32
V_DIM = 64
ROPE_THETA = 10000.0
G_D = 8
CHUNK = 128
PAGE_SIZE = 128
EPS = 1e-6
NEG_INF = -1e30
SCALE_B = (NOPE_DIM + ROPE_DIM) ** -0.5

LANES = 128
SUBLANES = 8
VMEM_LIMIT = 56 * 1024 * 1024

ROW_TILE = 512
ATTN_Q_TILE = 256
PAGES_PER_CHUNK = 32
PAGED_SLOTS = 4

F32 = jnp.float32
BF16 = jnp.bfloat16


def _rms(x, g):
    return x * lax.rsqrt(jnp.mean(x * x, axis=-1, keepdims=True) + EPS) * g


def _ln(x, g, b):
    xc = x - jnp.mean(x, axis=-1, keepdims=True)
    return xc * lax.rsqrt(jnp.mean(xc * xc, axis=-1, keepdims=True) + EPS) * g + b


def _silu(x):
    return x * jax.nn.sigmoid(x)


def _dot(a, b):
    return jnp.dot(a.astype(BF16), b.astype(BF16), preferred_element_type=F32)


def _dot_nt(a, b):
    return lax.dot_general(a.astype(BF16), b.astype(BF16), (((1,), (1,)), ((), ())),
                           preferred_element_type=F32)


def _params(*sem):
    return pltpu.CompilerParams(dimension_semantics=sem, vmem_limit_bytes=VMEM_LIMIT)


def _mod_kernel(c_ref, w_ref, b_ref, o_ref):
    o_ref[0] = _dot(_silu(c_ref[...]), w_ref[0]) + b_ref[0]


def _modulation(c_all, w_mod, b_mod):
    n_l, d, n = w_mod.shape
    bc = c_all.shape[0]
    tn = n // 4
    return pl.pallas_call(
        _mod_kernel,
        grid=(n_l, n // tn),
        in_specs=[pl.BlockSpec((bc, d), lambda l, j: (0, 0)),
                  pl.BlockSpec((1, d, tn), lambda l, j: (l, 0, j)),
                  pl.BlockSpec((1, 1, tn), lambda l, j: (l, 0, j))],
        out_specs=pl.BlockSpec((1, bc, tn), lambda l, j: (l, 0, j)),
        out_shape=jax.ShapeDtypeStruct((n_l, bc, n), F32),
        compiler_params=_params("arbitrary", "arbitrary"),
    )(c_all, w_mod, b_mod.reshape(n_l, 1, n))


def _mod_spec(bt, d, k, grid_rank):
    if grid_rank == 2:
        return pl.BlockSpec((bt, 1, d), lambda b, s: (b, 0, k))
    return pl.BlockSpec((bt, 1, d), lambda b, s, f: (b, 0, k))


def _const_spec(shape, grid_rank):
    zeros = (0,) * len(shape)
    if grid_rank == 2:
        return pl.BlockSpec(shape, lambda b, s: zeros)
    return pl.BlockSpec(shape, lambda b, s, f: zeros)


def _tiles(b, s):
    if s >= ROW_TILE:
        assert s % ROW_TILE == 0
        return 1, ROW_TILE
    assert ROW_TILE % s == 0 and s % SUBLANES == 0 and b % (ROW_TILE // s) == 0
    return ROW_TILE // s, s


def _ab_in_kernel(x_ref, sh_ref, sc_ref, g_ref, w_ref, cw_ref, hist_ref, qn_ref, wq_ref, kvn_ref,
                  wuk_ref, cosq_ref, sinq_ref, ropek_ref,
                  ya_ref, qcat_ref, ckv_ref, kpe_ref, kcat_ref, nconv_ref,
                  zs_ref, *, bt, ts, d_a, q_rank, kv_rank):
    tm = bt * ts
    hist_rows = nconv_ref.shape[1]
    x = x_ref[...]
    h = _rms(x, g_ref[...]) * (1.0 + sc_ref[...]) + sh_ref[...]
    proj = _dot(h.reshape(tm, h.shape[-1]), w_ref[...])
    o_cq = 3 * d_a
    o_ckv = o_cq + q_rank
    o_kpe = o_ckv + kv_rank
    gate_out = proj[:, 0:d_a].reshape(bt, ts, d_a)
    z = (proj[:, d_a:2 * d_a] * proj[:, 2 * d_a:3 * d_a]).reshape(bt, ts, d_a)

    base = SUBLANES - hist_rows

    @pl.when(pl.program_id(1) == 0)
    def _():
        zs_ref[:, base:SUBLANES, :] = hist_ref[...]

    zs_ref[:, SUBLANES:SUBLANES + ts, :] = z
    cw = cw_ref[...]
    conv = z * cw[hist_rows:hist_rows + 1, :]
    for k in range(hist_rows):
        conv = conv + zs_ref[:, base + k:base + k + ts, :] * cw[k:k + 1, :]
    ya_ref[...] = (gate_out * conv).reshape(tm, d_a).astype(ya_ref.dtype)
    tail = zs_ref[:, ts + base:ts + SUBLANES, :]
    nconv_ref[...] = tail
    zs_ref[:, base:SUBLANES, :] = tail

    cqn = _rms(proj[:, o_cq:o_ckv], qn_ref[...])
    q = _dot(cqn, wq_ref[...]) * SCALE_B
    n_nope = N_HEADS_B * NOPE_DIM
    n_rope = N_HEADS_B * ROPE_DIM
    qr = q[:, n_nope:n_nope + n_rope].reshape(bt, ts, n_rope)
    qs = q[:, n_nope + n_rope:n_nope + 2 * n_rope].reshape(bt, ts, n_rope)
    qpe = (qr * cosq_ref[...][None] + qs * sinq_ref[...][None]).reshape(tm, n_rope)
    for hh in range(N_HEADS_B):
        pair = q[:, LANES * (hh // 2):LANES * (hh // 2 + 1)]
        q_lat = _dot(pair, wuk_ref[hh])
        qcat_ref[hh] = jnp.concatenate(
            [q_lat, qpe[:, hh * ROPE_DIM:(hh + 1) * ROPE_DIM]], axis=-1).astype(qcat_ref.dtype)

    ckvn = _rms(proj[:, o_ckv:o_kpe], kvn_ref[...])
    ckv_ref[...] = ckvn
    kt = proj[:, o_kpe:o_kpe + LANES].reshape(bt, ts, LANES) * ropek_ref[...][None]
    kpe = (kt[:, :, 0:ROPE_DIM] + kt[:, :, ROPE_DIM:2 * ROPE_DIM]).reshape(tm, ROPE_DIM)
    kpe_ref[...] = kpe
    kcat_ref[...] = jnp.concatenate([ckvn, kpe], axis=-1).astype(BF16)


def _ab_in(x, mod3, g_pre, w_in_ext, conv_w, hist, q_norm, w_q_perm, kv_norm, w_uk_pad,
           cosq, sinq, ropek, ql_dtype):
    b, s, d = x.shape
    m = b * s
    bt, ts = _tiles(b, s)
    tm = bt * ts
    d_a = conv_w.shape[1]
    q_rank = q_norm.shape[-1]
    kv_rank = kv_norm.shape[-1]
    hist_rows = hist.shape[1]
    n_ext = w_in_ext.shape[1]
    grid = (b // bt, s // ts)
    row = lambda bb, ss: (bb * (s // ts) + ss, 0)
    kern = functools.partial(_ab_in_kernel, bt=bt, ts=ts, d_a=d_a, q_rank=q_rank, kv_rank=kv_rank)
    d_qk = kv_rank + ROPE_DIM
    out_shape = (
        jax.ShapeDtypeStruct((m, d_a), BF16),
        jax.ShapeDtypeStruct((N_HEADS_B, m, d_qk), ql_dtype),
        jax.ShapeDtypeStruct((m, kv_rank), F32),
        jax.ShapeDtypeStruct((m, ROPE_DIM), F32),
        jax.ShapeDtypeStruct((m, d_qk), BF16),
        jax.ShapeDtypeStruct((b, hist_rows, d_a), F32),
    )
    out_specs = (
        pl.BlockSpec((tm, d_a), row),
        pl.BlockSpec((N_HEADS_B, tm, d_qk), lambda bb, ss: (0, bb * (s // ts) + ss, 0)),
        pl.BlockSpec((tm, kv_rank), row),
        pl.BlockSpec((tm, ROPE_DIM), row),
        pl.BlockSpec((tm, d_qk), row),
        pl.BlockSpec((bt, hist_rows, d_a), lambda bb, ss: (bb, 0, 0)),
    )
    in_specs = [
        pl.BlockSpec((bt, ts, d), lambda bb, ss: (bb, ss, 0)),
        _mod_spec(bt, d, 0, 2), _mod_spec(bt, d, 1, 2),
        _const_spec((1, d), 2),
        _const_spec((d, n_ext), 2),
        _const_spec(conv_w.shape, 2),
        pl.BlockSpec((bt, hist_rows, d_a), lambda bb, ss: (bb, 0, 0)),
        _const_spec((1, q_rank), 2),
        _const_spec(w_q_perm.shape, 2),
        _const_spec((1, kv_rank), 2),
        _const_spec(w_uk_pad.shape, 2),
        pl.BlockSpec((ts, cosq.shape[1]), lambda bb, ss: (ss, 0)),
        pl.BlockSpec((ts, sinq.shape[1]), lambda bb, ss: (ss, 0)),
        pl.BlockSpec((ts, LANES), lambda bb, ss: (ss, 0)),
    ]
    return pl.pallas_call(
        kern, grid=grid, in_specs=in_specs, out_specs=out_specs, out_shape=out_shape,
        scratch_shapes=[pltpu.VMEM((bt, ts + SUBLANES, d_a), F32)],
        compiler_params=_params("arbitrary", "arbitrary"),
    )(x, mod3, mod3, g_pre.reshape(1, d), w_in_ext, conv_w, hist, q_norm.reshape(1, q_rank),
      w_q_perm, kv_norm.reshape(1, kv_rank), w_uk_pad, cosq, sinq, ropek)


def _fold_lanes(x, op):
    out = x[:, 0:LANES]
    for t in range(1, x.shape[1] // LANES):
        out = op(out, x[:, t * LANES:(t + 1) * LANES])
    return out


def _value_up(o, wuv_ref, rows):
    parts = []
    for j in range(N_HEADS_B // 2):
        o0 = o[(2 * j) * rows:(2 * j + 1) * rows]
        o1 = o[(2 * j + 1) * rows:(2 * j + 2) * rows]
        parts.append(_dot(o0, wuv_ref[2 * j]) + _dot(o1, wuv_ref[2 * j + 1]))
    return jnp.concatenate(parts, axis=-1)


def _prompt_attn_kernel(q_ref, k_ref, wuv_ref, yb_ref, s_ref, stat_ref, lsum_ref, acc_ref, *, tq, r):
    i = pl.program_id(1)
    rows = N_HEADS_B * tq
    q = q_ref[...].reshape(rows, q_ref.shape[-1])

    def key_block(j):
        return k_ref[pl.ds(pl.multiple_of(j * tq, tq), tq), :]

    def scores_pass(j, mp):
        s = _dot_nt(q, key_block(j))
        s_ref[j] = s
        return jnp.maximum(mp, _fold_lanes(s, jnp.maximum))

    stat_ref[...] = jnp.full(stat_ref.shape, NEG_INF, F32)

    def scores_body(j, carry):
        stat_ref[...] = scores_pass(j, stat_ref[...])
        return carry

    lax.fori_loop(0, i, scores_body, 0)
    s = _dot_nt(q, key_block(i))
    qpos = lax.broadcasted_iota(jnp.int32, s.shape, 0) & (tq - 1)
    kpos = lax.broadcasted_iota(jnp.int32, s.shape, 1)
    s = jnp.where(kpos <= qpos, s, NEG_INF)
    s_ref[i] = s
    mp = jnp.maximum(stat_ref[...], _fold_lanes(s, jnp.maximum))
    m = jnp.max(mp, axis=-1, keepdims=True)
    stat_ref[...] = jnp.broadcast_to(m, stat_ref.shape)
    acc_ref[...] = jnp.zeros(acc_ref.shape, F32)

    lsum_ref[...] = jnp.zeros(lsum_ref.shape, F32)

    def value_body(j, carry):
        mrow = stat_ref[...]
        p = jnp.exp(s_ref[j] - jnp.concatenate([mrow] * (tq // LANES), axis=-1))
        acc_ref[...] += jnp.dot(p.astype(BF16), key_block(j)[:, 0:r], preferred_element_type=F32)
        lsum_ref[...] += _fold_lanes(p, jnp.add)
        return carry

    lax.fori_loop(0, i + 1, value_body, 0)
    o = acc_ref[...] / jnp.sum(lsum_ref[...], axis=-1, keepdims=True)
    yb_ref[...] = _value_up(o, wuv_ref, tq).astype(yb_ref.dtype)


def _prompt_attention(q_cat, k_cat, w_uv_pad, b, s):
    n_h, m, d_qk = q_cat.shape
    r = d_qk - ROPE_DIM
    tq = ATTN_Q_TILE
    nq = s // tq
    rows = n_h * tq
    kern = functools.partial(_prompt_attn_kernel, tq=tq, r=r)
    return pl.pallas_call(
        kern, grid=(b, nq),
        in_specs=[pl.BlockSpec((n_h, tq, d_qk), lambda bb, i: (0, bb * nq + i, 0)),
                  pl.BlockSpec((s, d_qk), lambda bb, i: (bb, 0)),
                  _const_spec(w_uv_pad.shape, 2)],
        out_specs=pl.BlockSpec((tq, n_h * V_DIM), lambda bb, i: (bb * nq + i, 0)),
        out_shape=jax.ShapeDtypeStruct((m, n_h * V_DIM), BF16),
        scratch_shapes=[pltpu.VMEM((nq, rows, tq), F32), pltpu.VMEM((rows, LANES), F32),
                        pltpu.VMEM((rows, LANES), F32), pltpu.VMEM((rows, r), F32)],
        compiler_params=_params("arbitrary", "arbitrary"),
    )(q_cat, k_cat, w_uv_pad)


def _paged_attn_kernel(pt_ref, q_ref, cnew_ref, knew_ref, wuv_ref, ckv_hbm, kpet_hbm, yb_ref,
                       raw_k, raw_p, kb_ref, s_ref, sems, *, layer, n_chunks, n_slots, pages_chunk, page,
                       s_new, r):
    b = pl.program_id(0)
    keys_chunk = pages_chunk * page
    rows = N_HEADS_B * s_new
    qc = q_ref[...].reshape(rows, q_ref.shape[-1])
    q_lat = qc[:, 0:r].astype(BF16)
    q_pe = qc[:, r:].astype(BF16)

    def chunk_copies(seq, c, slot):
        copies = []
        for i in range(pages_chunk):
            pg = pt_ref[seq, c * pages_chunk + i]
            copies.append(pltpu.make_async_copy(
                ckv_hbm.at[layer, pg], raw_k.at[slot, pl.ds(i * page, page)], sems.at[0, slot]))
            copies.append(pltpu.make_async_copy(
                kpet_hbm.at[layer, pg], raw_p.at[slot, :, pl.ds(i * page, page)], sems.at[1, slot]))
        return copies

    def start_chunk(seq, c, slot):
        for n, cp in enumerate(chunk_copies(seq, c, slot)):
            cp.start(priority=(n // 2) % 2)

    ahead = n_slots - 1

    @pl.when(b == 0)
    def _():
        for c in range(ahead):
            start_chunk(b, c, c % n_slots)

    mp = jnp.full((rows, LANES), NEG_INF, F32)
    for c in range(n_chunks):
        slot = c % n_slots
        for cp in chunk_copies(b, c, slot):
            cp.wait()
        nxt = c + ahead
        if nxt < n_chunks:
            start_chunk(b, nxt, nxt % n_slots)
        else:
            @pl.when(b + 1 < pl.num_programs(0))
            def _():
                start_chunk(b + 1, nxt - n_chunks, nxt % n_slots)
        k = raw_k[slot].astype(BF16)
        kb_ref[c * keys_chunk:(c + 1) * keys_chunk, :] = k
        s = _dot_nt(q_lat, k) + jnp.dot(q_pe, raw_p[slot].astype(BF16), preferred_element_type=F32)
        s_ref[:, c * keys_chunk:(c + 1) * keys_chunk] = s
        mp = jnp.maximum(mp, _fold_lanes(s, jnp.maximum))

    k_new = cnew_ref[...].astype(BF16)
    s_n = _dot_nt(q_lat, k_new) + _dot_nt(q_pe, knew_ref[...])
    qpos = lax.broadcasted_iota(jnp.int32, s_n.shape, 0) & (s_new - 1)
    kpos = lax.broadcasted_iota(jnp.int32, s_n.shape, 1)
    s_n = jnp.where(kpos <= qpos, s_n, NEG_INF)
    m = jnp.maximum(jnp.max(mp, axis=-1, keepdims=True), jnp.max(s_n, axis=-1, keepdims=True))

    p_n = jnp.exp(s_n - m)
    acc = jnp.dot(p_n.astype(BF16), k_new, preferred_element_type=F32)
    lp = jnp.zeros((rows, LANES), F32)
    for c in range(n_chunks):
        p = jnp.exp(s_ref[:, c * keys_chunk:(c + 1) * keys_chunk] - m)
        lp = lp + _fold_lanes(p, jnp.add)
        acc = acc + jnp.dot(p.astype(BF16), kb_ref[c * keys_chunk:(c + 1) * keys_chunk, :],
                            preferred_element_type=F32)
    l = jnp.sum(lp, axis=-1, keepdims=True) + jnp.sum(p_n, axis=-1, keepdims=True)
    yb_ref[...] = _value_up(acc / l, wuv_ref, s_new).astype(yb_ref.dtype)


def _paged_attention(q_cat, ckv_new, kpe_new, cache_ckv, cache_kpe_t, page_table, layer, w_uv_pad, s_new):
    n_h, m, d_qk = q_cat.shape
    r = d_qk - ROPE_DIM
    b, n_pages = page_table.shape
    page = cache_ckv.shape[2]
    pages_chunk = min(PAGES_PER_CHUNK, n_pages)
    n_chunks = n_pages // pages_chunk
    n_slots = PAGED_SLOTS
    assert s_new & (s_new - 1) == 0 and n_pages % pages_chunk == 0 and n_chunks % n_slots == 0
    keys_chunk = pages_chunk * page
    rows = n_h * s_new
    kern = functools.partial(_paged_attn_kernel, layer=layer, n_chunks=n_chunks, n_slots=n_slots,
                             pages_chunk=pages_chunk, page=page, s_new=s_new, r=r)
    grid_spec = pltpu.PrefetchScalarGridSpec(
        num_scalar_prefetch=1, grid=(b,),
        in_specs=[pl.BlockSpec((n_h, s_new, d_qk), lambda bb, pt: (0, bb, 0)),
                  pl.BlockSpec((s_new, r), lambda bb, pt: (bb, 0)),
                  pl.BlockSpec((s_new, ROPE_DIM), lambda bb, pt: (bb, 0)),
                  pl.BlockSpec(w_uv_pad.shape, lambda bb, pt: (0, 0, 0)),
                  pl.BlockSpec(memory_space=pl.ANY),
                  pl.BlockSpec(memory_space=pl.ANY)],
        out_specs=pl.BlockSpec((s_new, n_h * V_DIM), lambda bb, pt: (bb, 0)),
        scratch_shapes=[pltpu.VMEM((n_slots, keys_chunk, r), F32),
                        pltpu.VMEM((n_slots, ROPE_DIM, keys_chunk), F32),
                        pltpu.VMEM((n_pages * page, r), BF16),
                        pltpu.VMEM((rows, n_pages * page), F32),
                        pltpu.SemaphoreType.DMA((2, n_slots))])
    return pl.pallas_call(
        kern, grid_spec=grid_spec,
        out_shape=jax.ShapeDtypeStruct((m, n_h * V_DIM), F32),
        compiler_params=_params("arbitrary"),
    )(page_table, q_cat, ckv_new, kpe_new, w_uv_pad, cache_ckv, cache_kpe_t)


def _cd_in_kernel(x_ref, sh_ref, sc_ref, g_ref, w_ref, cw_ref, cb_ref, lcg_ref, lcb_ref, lvg_ref, lvb_ref,
                  hist_ref, wsp_ref, bsp_ref, *rest, bt, ts, d_c, with_v):
    if with_v:
        yc_ref, yd_ref, nconv_ref, vn_ref, hs_ref = rest
    else:
        yc_ref, yd_ref, nconv_ref, hs_ref = rest
    tm = bt * ts
    hist_rows = nconv_ref.shape[1]
    n_taps = hist_rows + 1
    pad = hs_ref.shape[1] - ts
    base = pad - hist_rows
    x = x_ref[...]
    h = _rms(x, g_ref[...]) * (1.0 + sc_ref[...]) + sh_ref[...]
    proj = _dot(h.reshape(tm, h.shape[-1]), w_ref[...])

    glu = (proj[:, 0:d_c] * jax.nn.sigmoid(proj[:, d_c:2 * d_c])).reshape(bt, ts, d_c)

    @pl.when(pl.program_id(1) == 0)
    def _():
        if base:
            hs_ref[:, 0:base, :] = jnp.zeros((bt, base, d_c), F32)
        hs_ref[:, base:pad, :] = hist_ref[...]

    hs_ref[:, pad:pad + ts, :] = glu
    cw = cw_ref[...]
    conv = glu * cw[n_taps - 1:n_taps, :] + cb_ref[...]
    for rem in range(SUBLANES):
        part = None
        for row0 in range(0, pad, SUBLANES):
            k = row0 + rem - base
            if 0 <= k < n_taps - 1:
                term = hs_ref[:, row0:row0 + ts + SUBLANES, :] * cw[k:k + 1, :]
                part = term if part is None else part + term
        if part is not None:
            conv = conv + part[:, rem:rem + ts, :]
    yc = _silu(_ln(conv, lcg_ref[...], lcb_ref[...]))
    yc_ref[...] = yc.reshape(tm, d_c).astype(yc_ref.dtype)
    tail = hs_ref[:, ts + base:ts + pad, :]
    nconv_ref[...] = tail
    hs_ref[:, base:pad, :] = tail

    u = proj[:, 2 * d_c:3 * d_c]
    vn = _ln(proj[:, 3 * d_c:4 * d_c], lvg_ref[...], lvb_ref[...])
    if with_v:
        vn_ref[...] = vn
    low = lax.broadcasted_iota(jnp.int32, (CHUNK, LANES), 1) < (LANES // 2)
    bsp = bsp_ref[...]
    for c in range(tm // CHUNK):
        parts = []
        for j in range(d_c // LANES):
            vj = vn[c * CHUNK:(c + 1) * CHUNK, j * LANES:(j + 1) * LANES]
            rhs = jnp.concatenate([jnp.where(low, vj, 0.0), jnp.where(low, 0.0, vj)], axis=0)
            parts.append(_dot(wsp_ref[j], rhs))
        sv = jnp.concatenate(parts, axis=-1) + bsp
        yd_ref[c * CHUNK:(c + 1) * CHUNK, :] = (u[c * CHUNK:(c + 1) * CHUNK] * sv).astype(yd_ref.dtype)


def _cd_in(x, mod3, g_pre, w_in, conv_w, conv_b, ln_c_g, ln_c_b, ln_v_g, ln_v_b, hist, w_sp_cat, b_sp_rows,
           with_v):
    b, s, d = x.shape
    m = b * s
    bt, ts = _tiles(b, s)
    tm = bt * ts
    d_c = conv_w.shape[1]
    hist_rows = hist.shape[1]
    pad = -(-hist_rows // SUBLANES) * SUBLANES
    grid = (b // bt, s // ts)
    row = lambda bb, ss: (bb * (s // ts) + ss, 0)
    vec = lambda a: a.reshape(1, -1)
    kern = functools.partial(_cd_in_kernel, bt=bt, ts=ts, d_c=d_c, with_v=with_v)
    out_shape = [jax.ShapeDtypeStruct((m, d_c), BF16), jax.ShapeDtypeStruct((m, d_c), BF16),
                 jax.ShapeDtypeStruct((b, hist_rows, d_c), F32)]
    out_specs = [pl.BlockSpec((tm, d_c), row), pl.BlockSpec((tm, d_c), row),
                 pl.BlockSpec((bt, hist_rows, d_c), lambda bb, ss: (bb, 0, 0))]
    if with_v:
        out_shape.append(jax.ShapeDtypeStruct((m, d_c), F32))
        out_specs.append(pl.BlockSpec((tm, d_c), row))
    in_specs = [
        pl.BlockSpec((bt, ts, d), lambda bb, ss: (bb, ss, 0)),
        _mod_spec(bt, d, 0, 2), _mod_spec(bt, d, 1, 2),
        _const_spec((1, d), 2),
        _const_spec(w_in.shape, 2),
        _const_spec(conv_w.shape, 2),
        _const_spec((1, d_c), 2), _const_spec((1, d_c), 2), _const_spec((1, d_c), 2),
        _const_spec((1, d_c), 2), _const_spec((1, d_c), 2),
        pl.BlockSpec((bt, hist_rows, d_c), lambda bb, ss: (bb, 0, 0)),
        _const_spec(w_sp_cat.shape, 2),
        _const_spec(b_sp_rows.shape, 2),
    ]
    return pl.pallas_call(
        kern, grid=grid, in_specs=in_specs, out_specs=tuple(out_specs), out_shape=tuple(out_shape),
        scratch_shapes=[pltpu.VMEM((bt, ts + pad, d_c), F32)],
        compiler_params=_params("arbitrary", "arbitrary"),
    )(x, mod3, mod3, vec(g_pre), w_in, conv_w, vec(conv_b), vec(ln_c_g), vec(ln_c_b), vec(ln_v_g),
      vec(ln_v_b), hist, w_sp_cat, b_sp_rows)


def _out_ffn_kernel(y0_ref, y1_ref, x_ref, gm_ref, shf_ref, scf_ref, gf_ref, npost_ref, nfpre_ref, nfpost_ref,
                    wo_ref, wg_ref, wu_ref, wd_ref, o_ref, x1_ref, h_ref, acc_ref, *, bt, ts):
    f = pl.program_id(2)
    tm = bt * ts
    d = x_ref.shape[-1]
    k0 = y0_ref.shape[-1]

    @pl.when(f == 0)
    def _():
        y = _dot(y0_ref[...], wo_ref[0:k0, :]) + _dot(y1_ref[...], wo_ref[k0:, :])
        x1 = x_ref[...] + gm_ref[...] * _rms(y, npost_ref[...]).reshape(bt, ts, d)
        x1_ref[...] = x1
        h = _rms(x1, nfpre_ref[...]) * (1.0 + scf_ref[...]) + shf_ref[...]
        h_ref[...] = h.reshape(tm, d).astype(BF16)
        acc_ref[...] = jnp.zeros(acc_ref.shape, F32)

    h = h_ref[...]
    a = _silu(_dot(h, wg_ref[...])) * _dot(h, wu_ref[...])
    acc_ref[...] += _dot(a, wd_ref[...])

    @pl.when(f == pl.num_programs(2) - 1)
    def _():
        o_ref[...] = x1_ref[...] + gf_ref[...] * _rms(acc_ref[...], nfpost_ref[...]).reshape(bt, ts, d)


def _out_ffn(y0, y1, x, mod3, norm_post, norm_ffn_pre, norm_ffn_post, w_out, w_gate, w_up, w_down, layer):
    b, s, d = x.shape
    bt, ts = _tiles(b, s)
    tm = bt * ts
    d_ff = w_gate.shape[2]
    n_f = 2
    tf = d_ff // n_f
    assert tf % LANES == 0
    grid = (b // bt, s // ts, n_f)
    row = lambda bb, ss, f: (bb * (s // ts) + ss, 0)
    vec = lambda a: a.reshape(1, -1)
    kern = functools.partial(_out_ffn_kernel, bt=bt, ts=ts)
    in_specs = [
        pl.BlockSpec((tm, y0.shape[1]), row),
        pl.BlockSpec((tm, y1.shape[1]), row),
        pl.BlockSpec((bt, ts, d), lambda bb, ss, f: (bb, ss, 0)),
        _mod_spec(bt, d, 2, 3), _mod_spec(bt, d, 3, 3), _mod_spec(bt, d, 4, 3), _mod_spec(bt, d, 5, 3),
        _const_spec((1, d), 3), _const_spec((1, d), 3), _const_spec((1, d), 3),
        _const_spec(w_out.shape, 3),
        pl.BlockSpec((None, d, tf), lambda bb, ss, f: (layer, 0, f)),
        pl.BlockSpec((None, d, tf), lambda bb, ss, f: (layer, 0, f)),
        pl.BlockSpec((None, tf, d), lambda bb, ss, f: (layer, f, 0)),
    ]
    return pl.pallas_call(
        kern, grid=grid, in_specs=in_specs,
        out_specs=pl.BlockSpec((bt, ts, d), lambda bb, ss, f: (bb, ss, 0)),
        out_shape=jax.ShapeDtypeStruct((b, s, d), F32),
        scratch_shapes=[pltpu.VMEM((bt, ts, d), F32), pltpu.VMEM((tm, d), BF16), pltpu.VMEM((tm, d), F32)],
        compiler_params=_params("arbitrary", "arbitrary", "arbitrary"),
    )(y0, y1, x, mod3, mod3, mod3, mod3, vec(norm_post), vec(norm_ffn_pre), vec(norm_ffn_post),
      w_out, w_gate, w_up, w_down)


def _rope_tables(pos):
    inv_freq = ROPE_THETA ** (-jnp.arange(0, ROPE_DIM, 2, dtype=F32) / ROPE_DIM)
    ang = pos.astype(F32)[:, None] * inv_freq[None, :]
    cos, sin = jnp.cos(ang), jnp.sin(ang)
    cosq = jnp.tile(jnp.concatenate([cos, cos], axis=-1), (1, N_HEADS_B))
    sinq = jnp.tile(jnp.concatenate([-sin, sin], axis=-1), (1, N_HEADS_B))
    zeros = jnp.zeros((pos.shape[0], LANES - 2 * ROPE_DIM), F32)
    ropek = jnp.concatenate([cos, cos, -sin, sin, zeros], axis=-1)
    return cosq, sinq, ropek


def _prep_ab_weights(w_in, w_q_up, w_kv_up, d_a, q_rank, kv_rank):
    d = w_in.shape[0]
    o_kpe = 3 * d_a + q_rank + kv_rank
    half = ROPE_DIM // 2
    swapped = jnp.concatenate([w_in[:, o_kpe + half:o_kpe + ROPE_DIM], w_in[:, o_kpe:o_kpe + half]], axis=1)
    w_in_ext = jnp.concatenate(
        [w_in, swapped, jnp.zeros((d, LANES - 2 * ROPE_DIM), w_in.dtype)], axis=1).astype(BF16)
    wq = w_q_up.reshape(q_rank, N_HEADS_B, NOPE_DIM + ROPE_DIM)
    wq_rope = wq[:, :, NOPE_DIM:]
    wq_rope_sw = jnp.concatenate([wq_rope[:, :, half:], wq_rope[:, :, :half]], axis=2)
    w_q_perm = jnp.concatenate([
        wq[:, :, :NOPE_DIM].reshape(q_rank, -1),
        wq_rope.reshape(q_rank, -1),
        wq_rope_sw.reshape(q_rank, -1)], axis=1).astype(BF16)
    w_kv = w_kv_up.reshape(kv_rank, N_HEADS_B, NOPE_DIM + V_DIM)
    w_uk_t = jnp.transpose(w_kv[..., :NOPE_DIM], (1, 2, 0))
    w_uv = jnp.transpose(w_kv[..., NOPE_DIM:], (1, 0, 2))
    odd = (jnp.arange(N_HEADS_B) % 2 == 1)
    zk = jnp.zeros_like(w_uk_t)
    w_uk_pad = jnp.where(odd[:, None, None],
                         jnp.concatenate([zk, w_uk_t], axis=1),
                         jnp.concatenate([w_uk_t, zk], axis=1)).astype(BF16)
    zv = jnp.zeros_like(w_uv)
    w_uv_pad = jnp.where(odd[:, None, None],
                         jnp.concatenate([zv, w_uv], axis=2),
                         jnp.concatenate([w_uv, zv], axis=2)).astype(BF16)
    return w_in_ext, w_q_perm, w_uk_pad, w_uv_pad


def _prep_spatial(w_sp, b_sp, n):
    wm = jnp.tril(w_sp[:, :n, :n])
    reps = CHUNK // n
    eye = jnp.eye(reps, dtype=w_sp.dtype)
    big = jnp.einsum('ab,gts->gatbs', eye, wm).reshape(G_D, CHUNK, CHUNK)
    w_cat = jnp.concatenate([big[0::2], big[1::2]], axis=2).astype(BF16)
    b_rows = jnp.tile(b_sp[:, :n].T, (reps, 1))
    return w_cat, b_rows


def _trunk(x, mods, pos, hist_a, hist_c, attend, chunk_rows, with_v, ql_dtype, p):
    b, s, d = x.shape
    depth = p['w_ffn_gate'].shape[0]
    cosq, sinq, ropek = _rope_tables(pos)
    conv_a, ckvs, kpes, conv_c, vds = [], [], [], [], []
    for l in range(depth):
        mod3 = mods[l].reshape(b, 1, -1)
        i = l // 2
        if l % 2 == 0:
            w_in_ext, w_q_perm, w_uk_pad, w_uv_pad = p['ab'][i]
            ya, q_cat, ckv, kpe, k_cat, nconv = _ab_in(
                x, mod3, p['norm_mix_pre'][l], w_in_ext, p['conv_a_w'][i], hist_a[i], p['q_norm'][i],
                w_q_perm, p['kv_norm'][i], w_uk_pad, cosq, sinq, ropek, ql_dtype)
            y1 = attend(i, q_cat, ckv, kpe, k_cat, w_uv_pad)
            y0 = ya
            w_out = p['w_out_ab'][i]
            conv_a.append(nconv)
            ckvs.append(ckv.reshape(b, s, -1))
            kpes.append(kpe.reshape(b, s, -1))
        else:
            w_sp_cat, b_rows = _prep_spatial(p['w_spatial'][i], p['b_spatial'][i], chunk_rows)
            d_c = p['conv_c_w'].shape[-1]
            b_sp_rows = jnp.repeat(b_rows, d_c // G_D, axis=1)
            outs = _cd_in(x, mod3, p['norm_mix_pre'][l], p['w_in_cd'][i], p['conv_c_w'][i], p['conv_c_b'][i],
                          p['ln_c_g'][i], p['ln_c_b'][i], p['ln_v_g'][i], p['ln_v_b'][i], hist_c[i],
                          w_sp_cat, b_sp_rows, with_v)
            y0, y1, nconv = outs[:3]
            if with_v:
                vds.append(outs[3].reshape(b, s, -1))
            w_out = p['w_out_cd'][i]
            conv_c.append(nconv)
        x = _out_ffn(y0, y1, x, mod3, p['norm_mix_post'][l], p['norm_ffn_pre'][l], p['norm_ffn_post'][l],
                     w_out, p['w_ffn_gate'], p['w_ffn_up'], p['w_ffn_down'], l)
    return x, conv_a, ckvs, kpes, conv_c, vds


def kernel(x_prompt, x_sample, cache_ckv, cache_kpe, state_conv_a, state_conv_c, page_table, c_prompt, c_sample, w_mod, b_mod, norm_mix_pre, norm_mix_post, norm_ffn_pre, norm_ffn_post, w_in_ab, conv_a_w, q_norm, w_q_up, kv_norm, w_kv_up, w_out_ab, w_in_cd, conv_c_w, conv_c_b, ln_c_g, ln_c_b, ln_v_g, ln_v_b, w_spatial, b_spatial, w_out_cd, w_ffn_gate, w_ffn_up, w_ffn_down):
    b_p, s_p, d = x_prompt.shape
    b_s, s_s, _ = x_sample.shape
    n_ab, n_cd = w_in_ab.shape[0], w_in_cd.shape[0]
    d_a = conv_a_w.shape[-1]
    d_c = conv_c_w.shape[-1]
    q_rank, kv_rank = q_norm.shape[-1], kv_norm.shape[-1]

    p = {
        'norm_mix_pre': norm_mix_pre, 'norm_mix_post': norm_mix_post, 'norm_ffn_pre': norm_ffn_pre,
        'norm_ffn_post': norm_ffn_post, 'conv_a_w': conv_a_w, 'q_norm': q_norm, 'kv_norm': kv_norm,
        'w_out_ab': w_out_ab.astype(BF16), 'w_in_cd': w_in_cd.astype(BF16), 'conv_c_w': conv_c_w,
        'conv_c_b': conv_c_b, 'ln_c_g': ln_c_g, 'ln_c_b': ln_c_b, 'ln_v_g': ln_v_g, 'ln_v_b': ln_v_b,
        'w_spatial': w_spatial, 'b_spatial': b_spatial, 'w_out_cd': w_out_cd.astype(BF16),
        'w_ffn_gate': w_ffn_gate.astype(BF16), 'w_ffn_up': w_ffn_up.astype(BF16),
        'w_ffn_down': w_ffn_down.astype(BF16),
        'ab': [_prep_ab_weights(w_in_ab[i], w_q_up[i], w_kv_up[i], d_a, q_rank, kv_rank) for i in range(n_ab)],
    }

    mods = _modulation(jnp.concatenate([c_prompt, c_sample], axis=0), w_mod, b_mod)
    mods_p, mods_s = mods[:, :b_p], mods[:, b_p:]

    zeros_a = jnp.zeros((n_ab, b_p, state_conv_a.shape[2], d_a), F32)
    zeros_c = jnp.zeros((n_cd, b_p, state_conv_c.shape[2], d_c), F32)

    def attend_prompt(i, q_cat, ckv, kpe, k_cat, w_uv_pad):
        return _prompt_attention(q_cat, k_cat, w_uv_pad, b_p, s_p)

    y_p, sa_p, ckv_p, kpe_p, sc_p, _ = _trunk(
        x_prompt, mods_p, jnp.arange(s_p), zeros_a, zeros_c, attend_prompt, CHUNK, False, BF16, p)

    past_len = page_table.shape[1] * PAGE_SIZE

    cache_kpe_t = jnp.swapaxes(cache_kpe, 2, 3)

    def attend_sample(i, q_cat, ckv, kpe, k_cat, w_uv_pad):
        return _paged_attention(q_cat, ckv, kpe, cache_ckv, cache_kpe_t, page_table, i, w_uv_pad, s_s)

    y_s, sa_s, ckv_s, kpe_s, sc_s, vd_s = _trunk(
        x_sample, mods_s, past_len + jnp.arange(s_s), state_conv_a, state_conv_c, attend_sample, s_s,
        True, F32, p)

    st = jnp.stack
    return (y_p, y_s, st(sa_p), st(ckv_p), st(kpe_p), st(sc_p), st(sa_s), st(ckv_s), st(kpe_s), st(sc_s),
            st(vd_s))
```

```python
import functools

import jax
import jax.numpy as jnp
from jax import lax
from jax.experimental import pallas as pl
from jax.experimental.pallas import tpu as pltpu

N_HEADS_B = 8
NOPE_DIM = 64
ROPE_DIM = 32
V_DIM = 64
ROPE_THETA = 10000.0
G_D = 8
CHUNK = 128
PAGE_SIZE = 128
EPS = 1e-6
NEG_INF = -1e30
SCALE_B = (NOPE_DIM + ROPE_DIM) ** -0.5

LANES = 128
SUBLANES = 8
VMEM_LIMIT = 56 * 1024 * 1024

ROW_TILE = 512
ATTN_Q_TILE = 256
PAGES_PER_CHUNK = 32
PAGED_SLOTS = 4

F32 = jnp.float32
BF16 = jnp.bfloat16


def _rms(x, g):
    return x * lax.rsqrt(jnp.mean(x * x, axis=-1, keepdims=True) + EPS) * g


def _ln(x, g, b):
    xc = x - jnp.mean(x, axis=-1, keepdims=True)
    return xc * lax.rsqrt(jnp.mean(xc * xc, axis=-1, keepdims=True) + EPS) * g + b


def _silu(x):
    return x * jax.nn.sigmoid(x)


def _dot(a, b):
    return jnp.dot(a.astype(BF16), b.astype(BF16), preferred_element_type=F32)


def _dot_nt(a, b):
    return lax.dot_general(a.astype(BF16), b.astype(BF16), (((1,), (1,)), ((), ())),
                           preferred_element_type=F32)


def _params(*sem):
    return pltpu.CompilerParams(dimension_semantics=sem, vmem_limit_bytes=VMEM_LIMIT)


def _mod_kernel(c_ref, w_ref, b_ref, o_ref):
    o_ref[0] = _dot(_silu(c_ref[...]), w_ref[0]) + b_ref[0]


def _modulation(c_all, w_mod, b_mod):
    n_l, d, n = w_mod.shape
    bc = c_all.shape[0]
    tn = n // 4
    return pl.pallas_call(
        _mod_kernel,
        grid=(n_l, n // tn),
        in_specs=[pl.BlockSpec((bc, d), lambda l, j: (0, 0)),
                  pl.BlockSpec((1, d, tn), lambda l, j: (l, 0, j)),
                  pl.BlockSpec((1, 1, tn), lambda l, j: (l, 0, j))],
        out_specs=pl.BlockSpec((1, bc, tn), lambda l, j: (l, 0, j)),
        out_shape=jax.ShapeDtypeStruct((n_l, bc, n), F32),
        compiler_params=_params("arbitrary", "arbitrary"),
    )(c_all, w_mod, b_mod.reshape(n_l, 1, n))


def _mod_spec(bt, d, k, grid_rank):
    if grid_rank == 2:
        return pl.BlockSpec((bt, 1, d), lambda b, s: (b, 0, k))
    return pl.BlockSpec((bt, 1, d), lambda b, s, f: (b, 0, k))


def _const_spec(shape, grid_rank):
    zeros = (0,) * len(shape)
    if grid_rank == 2:
        return pl.BlockSpec(shape, lambda b, s: zeros)
    return pl.BlockSpec(shape, lambda b, s, f: zeros)


def _tiles(b, s):
    if s >= ROW_TILE:
        assert s % ROW_TILE == 0
        return 1, ROW_TILE
    assert ROW_TILE % s == 0 and s % SUBLANES == 0 and b % (ROW_TILE // s) == 0
    return ROW_TILE // s, s


def _ab_in_kernel(x_ref, sh_ref, sc_ref, g_ref, w_ref, cw_ref, hist_ref, qn_ref, wq_ref, kvn_ref,
                  wuk_ref, cosq_ref, sinq_ref, ropek_ref,
                  ya_ref, qcat_ref, ckv_ref, kpe_ref, kcat_ref, nconv_ref,
                  zs_ref, *, bt, ts, d_a, q_rank, kv_rank):
    tm = bt * ts
    hist_rows = nconv_ref.shape[1]
    x = x_ref[...]
    h = _rms(x, g_ref[...]) * (1.0 + sc_ref[...]) + sh_ref[...]
    proj = _dot(h.reshape(tm, h.shape[-1]), w_ref[...])
    o_cq = 3 * d_a
    o_ckv = o_cq + q_rank
    o_kpe = o_ckv + kv_rank
    gate_out = proj[:, 0:d_a].reshape(bt, ts, d_a)
    z = (proj[:, d_a:2 * d_a] * proj[:, 2 * d_a:3 * d_a]).reshape(bt, ts, d_a)

    base = SUBLANES - hist_rows

    @pl.when(pl.program_id(1) == 0)
    def _():
        zs_ref[:, base:SUBLANES, :] = hist_ref[...]

    zs_ref[:, SUBLANES:SUBLANES + ts, :] = z
    cw = cw_ref[...]
    conv = z * cw[hist_rows:hist_rows + 1, :]
    for k in range(hist_rows):
        conv = conv + zs_ref[:, base + k:base + k + ts, :] * cw[k:k + 1, :]
    ya_ref[...] = (gate_out * conv).reshape(tm, d_a).astype(ya_ref.dtype)
    tail = zs_ref[:, ts + base:ts + SUBLANES, :]
    nconv_ref[...] = tail
    zs_ref[:, base:SUBLANES, :] = tail

    cqn = _rms(proj[:, o_cq:o_ckv], qn_ref[...])
    q = _dot(cqn, wq_ref[...]) * SCALE_B
    n_nope = N_HEADS_B * NOPE_DIM
    n_rope = N_HEADS_B * ROPE_DIM
    qr = q[:, n_nope:n_nope + n_rope].reshape(bt, ts, n_rope)
    qs = q[:, n_nope + n_rope:n_nope + 2 * n_rope].reshape(bt, ts, n_rope)
    qpe = (qr * cosq_ref[...][None] + qs * sinq_ref[...][None]).reshape(tm, n_rope)
    for hh in range(N_HEADS_B):
        pair = q[:, LANES * (hh // 2):LANES * (hh // 2 + 1)]
        q_lat = _dot(pair, wuk_ref[hh])
        qcat_ref[hh] = jnp.concatenate(
            [q_lat, qpe[:, hh * ROPE_DIM:(hh + 1) * ROPE_DIM]], axis=-1).astype(qcat_ref.dtype)

    ckvn = _rms(proj[:, o_ckv:o_kpe], kvn_ref[...])
    ckv_ref[...] = ckvn
    kt = proj[:, o_kpe:o_kpe + LANES].reshape(bt, ts, LANES) * ropek_ref[...][None]
    kpe = (kt[:, :, 0:ROPE_DIM] + kt[:, :, ROPE_DIM:2 * ROPE_DIM]).reshape(tm, ROPE_DIM)
    kpe_ref[...] = kpe
    kcat_ref[...] = jnp.concatenate([ckvn, kpe], axis=-1).astype(BF16)


def _ab_in(x, mod3, g_pre, w_in_ext, conv_w, hist, q_norm, w_q_perm, kv_norm, w_uk_pad,
           cosq, sinq, ropek, ql_dtype):
    b, s, d = x.shape
    m = b * s
    bt, ts = _tiles(b, s)
    tm = bt * ts
    d_a = conv_w.shape[1]
    q_rank = q_norm.shape[-1]
    kv_rank = kv_norm.shape[-1]
    hist_rows = hist.shape[1]
    n_ext = w_in_ext.shape[1]
    grid = (b // bt, s // ts)
    row = lambda bb, ss: (bb * (s // ts) + ss, 0)
    kern = functools.partial(_ab_in_kernel, bt=bt, ts=ts, d_a=d_a, q_rank=q_rank, kv_rank=kv_rank)
    d_qk = kv_rank + ROPE_DIM
    out_shape = (
        jax.ShapeDtypeStruct((m, d_a), BF16),
        jax.ShapeDtypeStruct((N_HEADS_B, m, d_qk), ql_dtype),
        jax.ShapeDtypeStruct((m, kv_rank), F32),
        jax.ShapeDtypeStruct((m, ROPE_DIM), F32),
        jax.ShapeDtypeStruct((m, d_qk), BF16),
        jax.ShapeDtypeStruct((b, hist_rows, d_a), F32),
    )
    out_specs = (
        pl.BlockSpec((tm, d_a), row),
        pl.BlockSpec((N_HEADS_B, tm, d_qk), lambda bb, ss: (0, bb * (s // ts) + ss, 0)),
        pl.BlockSpec((tm, kv_rank), row),
        pl.BlockSpec((tm, ROPE_DIM), row),
        pl.BlockSpec((tm, d_qk), row),
        pl.BlockSpec((bt, hist_rows, d_a), lambda bb, ss: (bb, 0, 0)),
    )
    in_specs = [
        pl.BlockSpec((bt, ts, d), lambda bb, ss: (bb, ss, 0)),
        _mod_spec(bt, d, 0, 2), _mod_spec(bt, d, 1, 2),
        _const_spec((1, d), 2),
        _const_spec((d, n_ext), 2),
        _const_spec(conv_w.shape, 2),
        pl.BlockSpec((bt, hist_rows, d_a), lambda bb, ss: (bb, 0, 0)),
        _const_spec((1, q_rank), 2),
        _const_spec(w_q_perm.shape, 2),
        _const_spec((1, kv_rank), 2),
        _const_spec(w_uk_pad.shape, 2),
        pl.BlockSpec((ts, cosq.shape[1]), lambda bb, ss: (ss, 0)),
        pl.BlockSpec((ts, sinq.shape[1]), lambda bb, ss: (ss, 0)),
        pl.BlockSpec((ts, LANES), lambda bb, ss: (ss, 0)),
    ]
    return pl.pallas_call(
        kern, grid=grid, in_specs=in_specs, out_specs=out_specs, out_shape=out_shape,
        scratch_shapes=[pltpu.VMEM((bt, ts + SUBLANES, d_a), F32)],
        compiler_params=_params("arbitrary", "arbitrary"),
    )(x, mod3, mod3, g_pre.reshape(1, d), w_in_ext, conv_w, hist, q_norm.reshape(1, q_rank),
      w_q_perm, kv_norm.reshape(1, kv_rank), w_uk_pad, cosq, sinq, ropek)


def _fold_lanes(x, op):
    out = x[:, 0:LANES]
    for t in range(1, x.shape[1] // LANES):
        out = op(out, x[:, t * LANES:(t + 1) * LANES])
    return out


def _value_up(o, wuv_ref, rows):
    parts = []
    for j in range(N_HEADS_B // 2):
        o0 = o[(2 * j) * rows:(2 * j + 1) * rows]
        o1 = o[(2 * j + 1) * rows:(2 * j + 2) * rows]
        parts.append(_dot(o0, wuv_ref[2 * j]) + _dot(o1, wuv_ref[2 * j + 1]))
    return jnp.concatenate(parts, axis=-1)


def _prompt_attn_kernel(q_ref, k_ref, wuv_ref, yb_ref, s_ref, stat_ref, lsum_ref, acc_ref, *, tq, r):
    i = pl.program_id(1)
    rows = N_HEADS_B * tq
    q = q_ref[...].reshape(rows, q_ref.shape[-1])

    def key_block(j):
        return k_ref[pl.ds(pl.multiple_of(j * tq, tq), tq), :]

    def scores_pass(j, mp):
        s = _dot_nt(q, key_block(j))
        s_ref[j] = s
        return jnp.maximum(mp, _fold_lanes(s, jnp.maximum))

    stat_ref[...] = jnp.full(stat_ref.shape, NEG_INF, F32)

    def scores_body(j, carry):
        stat_ref[...] = scores_pass(j, stat_ref[...])
        return carry

    lax.fori_loop(0, i, scores_body, 0)
    s = _dot_nt(q, key_block(i))
    qpos = lax.broadcasted_iota(jnp.int32, s.shape, 0) & (tq - 1)
    kpos = lax.broadcasted_iota(jnp.int32, s.shape, 1)
    s = jnp.where(kpos <= qpos, s, NEG_INF)
    s_ref[i] = s
    mp = jnp.maximum(stat_ref[...], _fold_lanes(s, jnp.maximum))
    m = jnp.max(mp, axis=-1, keepdims=True)
    stat_ref[...] = jnp.broadcast_to(m, stat_ref.shape)
    acc_ref[...] = jnp.zeros(acc_ref.shape, F32)

    lsum_ref[...] = jnp.zeros(lsum_ref.shape, F32)

    def value_body(j, carry):
        mrow = stat_ref[...]
        p = jnp.exp(s_ref[j] - jnp.concatenate([mrow] * (tq // LANES), axis=-1))
        acc_ref[...] += jnp.dot(p.astype(BF16), key_block(j)[:, 0:r], preferred_element_type=F32)
        lsum_ref[...] += _fold_lanes(p, jnp.add)
        return carry

    lax.fori_loop(0, i + 1, value_body, 0)
    o = acc_ref[...] / jnp.sum(lsum_ref[...], axis=-1, keepdims=True)
    yb_ref[...] = _value_up(o, wuv_ref, tq).astype(yb_ref.dtype)


def _prompt_attention(q_cat, k_cat, w_uv_pad, b, s):
    n_h, m, d_qk = q_cat.shape
    r = d_qk - ROPE_DIM
    tq = ATTN_Q_TILE
    nq = s // tq
    rows = n_h * tq
    kern = functools.partial(_prompt_attn_kernel, tq=tq, r=r)
    return pl.pallas_call(
        kern, grid=(b, nq),
        in_specs=[pl.BlockSpec((n_h, tq, d_qk), lambda bb, i: (0, bb * nq + i, 0)),
                  pl.BlockSpec((s, d_qk), lambda bb, i: (bb, 0)),
                  _const_spec(w_uv_pad.shape, 2)],
        out_specs=pl.BlockSpec((tq, n_h * V_DIM), lambda bb, i: (bb * nq + i, 0)),
        out_shape=jax.ShapeDtypeStruct((m, n_h * V_DIM), BF16),
        scratch_shapes=[pltpu.VMEM((nq, rows, tq), F32), pltpu.VMEM((rows, LANES), F32),
                        pltpu.VMEM((rows, LANES), F32), pltpu.VMEM((rows, r), F32)],
        compiler_params=_params("arbitrary", "arbitrary"),
    )(q_cat, k_cat, w_uv_pad)


def _paged_attn_kernel(pt_ref, q_ref, cnew_ref, knew_ref, wuv_ref, ckv_hbm, kpet_hbm, yb_ref,
                       raw_k, raw_p, kb_ref, s_ref, sems, *, layer, n_chunks, n_slots, pages_chunk, page,
                       s_new, r):
    b = pl.program_id(0)
    keys_chunk = pages_chunk * page
    rows = N_HEADS_B * s_new
    qc = q_ref[...].reshape(rows, q_ref.shape[-1])
    q_lat = qc[:, 0:r].astype(BF16)
    q_pe = qc[:, r:].astype(BF16)

    def chunk_copies(seq, c, slot):
        copies = []
        for i in range(pages_chunk):
            pg = pt_ref[seq, c * pages_chunk + i]
            copies.append(pltpu.make_async_copy(
                ckv_hbm.at[layer, pg], raw_k.at[slot, pl.ds(i * page, page)], sems.at[0, slot]))
            copies.append(pltpu.make_async_copy(
                kpet_hbm.at[layer, pg], raw_p.at[slot, :, pl.ds(i * page, page)], sems.at[1, slot]))
        return copies

    def start_chunk(seq, c, slot):
        for cp in chunk_copies(seq, c, slot):
            cp.start()

    ahead = n_slots - 1

    @pl.when(b == 0)
    def _():
        for c in range(ahead):
            start_chunk(b, c, c % n_slots)

    parts = []

    def finish(c):
        s = s_ref[c % 2]
        m_c = jnp.max(_fold_lanes(s, jnp.maximum), axis=-1, keepdims=True)
        p = jnp.exp(s - m_c)
        l_c = jnp.sum(_fold_lanes(p, jnp.add), axis=-1, keepdims=True)
        o_c = jnp.dot(p.astype(BF16), kb_ref[c % 2], preferred_element_type=F32)
        parts.append((m_c, l_c, o_c))

    for c in range(n_chunks):
        slot = c % n_slots
        pltpu.make_async_copy(raw_k.at[slot], raw_k.at[slot], sems.at[0, slot]).wait()
        pltpu.make_async_copy(raw_p.at[slot], raw_p.at[slot], sems.at[1, slot]).wait()
        nxt = c + ahead
        if nxt < n_chunks:
            start_chunk(b, nxt, nxt % n_slots)
        else:
            @pl.when(b + 1 < pl.num_programs(0))
            def _():
                start_chunk(b + 1, nxt - n_chunks, nxt % n_slots)
        k = raw_k[slot].astype(BF16)
        kb_ref[c % 2] = k
        s_ref[c % 2] = (_dot_nt(q_lat, k)
                        + jnp.dot(q_pe, raw_p[slot].astype(BF16), preferred_element_type=F32))
        if c:
            finish(c - 1)
    finish(n_chunks - 1)

    k_new = cnew_ref[...].astype(BF16)
    s_n = _dot_nt(q_lat, k_new) + _dot_nt(q_pe, knew_ref[...])
    qpos = lax.broadcasted_iota(jnp.int32, s_n.shape, 0) & (s_new - 1)
    kpos = lax.broadcasted_iota(jnp.int32, s_n.shape, 1)
    s_n = jnp.where(kpos <= qpos, s_n, NEG_INF)
    m_n = jnp.max(s_n, axis=-1, keepdims=True)
    p_n = jnp.exp(s_n - m_n)
    parts.append((m_n, jnp.sum(p_n, axis=-1, keepdims=True),
                  jnp.dot(p_n.astype(BF16), k_new, preferred_element_type=F32)))

    m = parts[0][0]
    for m_c, _, _ in parts[1:]:
        m = jnp.maximum(m, m_c)
    l = jnp.zeros((rows, 1), F32)
    acc = jnp.zeros((rows, r), F32)
    for m_c, l_c, o_c in parts:
        w = jnp.exp(m_c - m)
        l = l + w * l_c
        acc = acc + w * o_c
    yb_ref[...] = _value_up(acc / l, wuv_ref, s_new).astype(yb_ref.dtype)


def _paged_attention(q_cat, ckv_new, kpe_new, cache_ckv, cache_kpe_t, page_table, layer, w_uv_pad, s_new):
    n_h, m, d_qk = q_cat.shape
    r = d_qk - ROPE_DIM
    b, n_pages = page_table.shape
    page = cache_ckv.shape[2]
    pages_chunk = min(PAGES_PER_CHUNK, n_pages)
    n_chunks = n_pages // pages_chunk
    n_slots = PAGED_SLOTS
    assert s_new & (s_new - 1) == 0 and n_pages % pages_chunk == 0 and n_chunks % n_slots == 0
    keys_chunk = pages_chunk * page
    rows = n_h * s_new
    kern = functools.partial(_paged_attn_kernel, layer=layer, n_chunks=n_chunks, n_slots=n_slots,
                             pages_chunk=pages_chunk, page=page, s_new=s_new, r=r)
    grid_spec = pltpu.PrefetchScalarGridSpec(
        num_scalar_prefetch=1, grid=(b,),
        in_specs=[pl.BlockSpec((n_h, s_new, d_qk), lambda bb, pt: (0, bb, 0)),
                  pl.BlockSpec((s_new, r), lambda bb, pt: (bb, 0)),
                  pl.BlockSpec((s_new, ROPE_DIM), lambda bb, pt: (bb, 0)),
                  pl.BlockSpec(w_uv_pad.shape, lambda bb, pt: (0, 0, 0)),
                  pl.BlockSpec(memory_space=pl.ANY),
                  pl.BlockSpec(memory_space=pl.ANY)],
        out_specs=pl.BlockSpec((s_new, n_h * V_DIM), lambda bb, pt: (bb, 0)),
        scratch_shapes=[pltpu.VMEM((n_slots, keys_chunk, r), F32),
                        pltpu.VMEM((n_slots, ROPE_DIM, keys_chunk), F32),
                        pltpu.VMEM((2, keys_chunk, r), BF16),
                        pltpu.VMEM((2, rows, keys_chunk), F32),
                        pltpu.SemaphoreType.DMA((2, n_slots))])
    return pl.pallas_call(
        kern, grid_spec=grid_spec,
        out_shape=jax.ShapeDtypeStruct((m, n_h * V_DIM), F32),
        compiler_params=_params("arbitrary"),
    )(page_table, q_cat, ckv_new, kpe_new, w_uv_pad, cache_ckv, cache_kpe_t)


def _cd_in_kernel(x_ref, sh_ref, sc_ref, g_ref, w_ref, cw_ref, cb_ref, lcg_ref, lcb_ref, lvg_ref, lvb_ref,
                  hist_ref, wsp_ref, bsp_ref, *rest, bt, ts, d_c, with_v):
    if with_v:
        yc_ref, yd_ref, nconv_ref, vn_ref, hs_ref = rest
    else:
        yc_ref, yd_ref, nconv_ref, hs_ref = rest
    tm = bt * ts
    hist_rows = nconv_ref.shape[1]
    n_taps = hist_rows + 1
    pad = hs_ref.shape[1] - ts
    base = pad - hist_rows
    x = x_ref[...]
    h = _rms(x, g_ref[...]) * (1.0 + sc_ref[...]) + sh_ref[...]
    proj = _dot(h.reshape(tm, h.shape[-1]), w_ref[...])

    glu = (proj[:, 0:d_c] * jax.nn.sigmoid(proj[:, d_c:2 * d_c])).reshape(bt, ts, d_c)

    @pl.when(pl.program_id(1) == 0)
    def _():
        if base:
            hs_ref[:, 0:base, :] = jnp.zeros((bt, base, d_c), F32)
        hs_ref[:, base:pad, :] = hist_ref[...]

    hs_ref[:, pad:pad + ts, :] = glu
    cw = cw_ref[...]
    conv = glu * cw[n_taps - 1:n_taps, :] + cb_ref[...]
    for rem in range(SUBLANES):
        part = None
        for row0 in range(0, pad, SUBLANES):
            k = row0 + rem - base
            if 0 <= k < n_taps - 1:
                term = hs_ref[:, row0:row0 + ts + SUBLANES, :] * cw[k:k + 1, :]
                part = term if part is None else part + term
        if part is not None:
            conv = conv + part[:, rem:rem + ts, :]
    yc = _silu(_ln(conv, lcg_ref[...], lcb_ref[...]))
    yc_ref[...] = yc.reshape(tm, d_c).astype(yc_ref.dtype)
    tail = hs_ref[:, ts + base:ts + pad, :]
    nconv_ref[...] = tail
    hs_ref[:, base:pad, :] = tail

    u = proj[:, 2 * d_c:3 * d_c]
    vn = _ln(proj[:, 3 * d_c:4 * d_c], lvg_ref[...], lvb_ref[...])
    if with_v:
        vn_ref[...] = vn
    low = lax.broadcasted_iota(jnp.int32, (CHUNK, LANES), 1) < (LANES // 2)
    bsp = bsp_ref[...]
    for c in range(tm // CHUNK):
        parts = []
        for j in range(d_c // LANES):
            vj = vn[c * CHUNK:(c + 1) * CHUNK, j * LANES:(j + 1) * LANES]
            rhs = jnp.concatenate([jnp.where(low, vj, 0.0), jnp.where(low, 0.0, vj)], axis=0)
            parts.append(_dot(wsp_ref[j], rhs))
        sv = jnp.concatenate(parts, axis=-1) + bsp
        yd_ref[c * CHUNK:(c + 1) * CHUNK, :] = (u[c * CHUNK:(c + 1) * CHUNK] * sv).astype(yd_ref.dtype)


def _cd_in(x, mod3, g_pre, w_in, conv_w, conv_b, ln_c_g, ln_c_b, ln_v_g, ln_v_b, hist, w_sp_cat, b_sp_rows,
           with_v):
    b, s, d = x.shape
    m = b * s
    bt, ts = _tiles(b, s)
    tm = bt * ts
    d_c = conv_w.shape[1]
    hist_rows = hist.shape[1]
    pad = -(-hist_rows // SUBLANES) * SUBLANES
    grid = (b // bt, s // ts)
    row = lambda bb, ss: (bb * (s // ts) + ss, 0)
    vec = lambda a: a.reshape(1, -1)
    kern = functools.partial(_cd_in_kernel, bt=bt, ts=ts, d_c=d_c, with_v=with_v)
    out_shape = [jax.ShapeDtypeStruct((m, d_c), BF16), jax.ShapeDtypeStruct((m, d_c), BF16),
                 jax.ShapeDtypeStruct((b, hist_rows, d_c), F32)]
    out_specs = [pl.BlockSpec((tm, d_c), row), pl.BlockSpec((tm, d_c), row),
                 pl.BlockSpec((bt, hist_rows, d_c), lambda bb, ss: (bb, 0, 0))]
    if with_v:
        out_shape.append(jax.ShapeDtypeStruct((m, d_c), F32))
        out_specs.append(pl.BlockSpec((tm, d_c), row))
    in_specs = [
        pl.BlockSpec((bt, ts, d), lambda bb, ss: (bb, ss, 0)),
        _mod_spec(bt, d, 0, 2), _mod_spec(bt, d, 1, 2),
        _const_spec((1, d), 2),
        _const_spec(w_in.shape, 2),
        _const_spec(conv_w.shape, 2),
        _const_spec((1, d_c), 2), _const_spec((1, d_c), 2), _const_spec((1, d_c), 2),
        _const_spec((1, d_c), 2), _const_spec((1, d_c), 2),
        pl.BlockSpec((bt, hist_rows, d_c), lambda bb, ss: (bb, 0, 0)),
        _const_spec(w_sp_cat.shape, 2),
        _const_spec(b_sp_rows.shape, 2),
    ]
    return pl.pallas_call(
        kern, grid=grid, in_specs=in_specs, out_specs=tuple(out_specs), out_shape=tuple(out_shape),
        scratch_shapes=[pltpu.VMEM((bt, ts + pad, d_c), F32)],
        compiler_params=_params("arbitrary", "arbitrary"),
    )(x, mod3, mod3, vec(g_pre), w_in, conv_w, vec(conv_b), vec(ln_c_g), vec(ln_c_b), vec(ln_v_g),
      vec(ln_v_b), hist, w_sp_cat, b_sp_rows)


def _out_ffn_kernel(y0_ref, y1_ref, x_ref, gm_ref, shf_ref, scf_ref, gf_ref, npost_ref, nfpre_ref, nfpost_ref,
                    wo_ref, wg_ref, wu_ref, wd_ref, o_ref, x1_ref, h_ref, acc_ref, *, bt, ts):
    f = pl.program_id(2)
    tm = bt * ts
    d = x_ref.shape[-1]
    k0 = y0_ref.shape[-1]

    @pl.when(f == 0)
    def _():
        y = _dot(y0_ref[...], wo_ref[0:k0, :]) + _dot(y1_ref[...], wo_ref[k0:, :])
        x1 = x_ref[...] + gm_ref[...] * _rms(y, npost_ref[...]).reshape(bt, ts, d)
        x1_ref[...] = x1
        h = _rms(x1, nfpre_ref[...]) * (1.0 + scf_ref[...]) + shf_ref[...]
        h_ref[...] = h.reshape(tm, d).astype(BF16)
        acc_ref[...] = jnp.zeros(acc_ref.shape, F32)

    h = h_ref[...]
    a = _silu(_dot(h, wg_ref[...])) * _dot(h, wu_ref[...])
    acc_ref[...] += _dot(a, wd_ref[...])

    @pl.when(f == pl.num_programs(2) - 1)
    def _():
        o_ref[...] = x1_ref[...] + gf_ref[...] * _rms(acc_ref[...], nfpost_ref[...]).reshape(bt, ts, d)


def _out_ffn(y0, y1, x, mod3, norm_post, norm_ffn_pre, norm_ffn_post, w_out, w_gate, w_up, w_down, layer):
    b, s, d = x.shape
    bt, ts = _tiles(b, s)
    tm = bt * ts
    d_ff = w_gate.shape[2]
    n_f = 2
    tf = d_ff // n_f
    assert tf % LANES == 0
    grid = (b // bt, s // ts, n_f)
    row = lambda bb, ss, f: (bb * (s // ts) + ss, 0)
    vec = lambda a: a.reshape(1, -1)
    kern = functools.partial(_out_ffn_kernel, bt=bt, ts=ts)
    in_specs = [
        pl.BlockSpec((tm, y0.shape[1]), row),
        pl.BlockSpec((tm, y1.shape[1]), row),
        pl.BlockSpec((bt, ts, d), lambda bb, ss, f: (bb, ss, 0)),
        _mod_spec(bt, d, 2, 3), _mod_spec(bt, d, 3, 3), _mod_spec(bt, d, 4, 3), _mod_spec(bt, d, 5, 3),
        _const_spec((1, d), 3), _const_spec((1, d), 3), _const_spec((1, d), 3),
        _const_spec(w_out.shape, 3),
        pl.BlockSpec((None, d, tf), lambda bb, ss, f: (layer, 0, f)),
        pl.BlockSpec((None, d, tf), lambda bb, ss, f: (layer, 0, f)),
        pl.BlockSpec((None, tf, d), lambda bb, ss, f: (layer, f, 0)),
    ]
    return pl.pallas_call(
        kern, grid=grid, in_specs=in_specs,
        out_specs=pl.BlockSpec((bt, ts, d), lambda bb, ss, f: (bb, ss, 0)),
        out_shape=jax.ShapeDtypeStruct((b, s, d), F32),
        scratch_shapes=[pltpu.VMEM((bt, ts, d), F32), pltpu.VMEM((tm, d), BF16), pltpu.VMEM((tm, d), F32)],
        compiler_params=_params("arbitrary", "arbitrary", "arbitrary"),
    )(y0, y1, x, mod3, mod3, mod3, mod3, vec(norm_post), vec(norm_ffn_pre), vec(norm_ffn_post),
      w_out, w_gate, w_up, w_down)


def _rope_tables(pos):
    inv_freq = ROPE_THETA ** (-jnp.arange(0, ROPE_DIM, 2, dtype=F32) / ROPE_DIM)
    ang = pos.astype(F32)[:, None] * inv_freq[None, :]
    cos, sin = jnp.cos(ang), jnp.sin(ang)
    cosq = jnp.tile(jnp.concatenate([cos, cos], axis=-1), (1, N_HEADS_B))
    sinq = jnp.tile(jnp.concatenate([-sin, sin], axis=-1), (1, N_HEADS_B))
    zeros = jnp.zeros((pos.shape[0], LANES - 2 * ROPE_DIM), F32)
    ropek = jnp.concatenate([cos, cos, -sin, sin, zeros], axis=-1)
    return cosq, sinq, ropek


def _prep_ab_weights(w_in, w_q_up, w_kv_up, d_a, q_rank, kv_rank):
    d = w_in.shape[0]
    o_kpe = 3 * d_a + q_rank + kv_rank
    half = ROPE_DIM // 2
    swapped = jnp.concatenate([w_in[:, o_kpe + half:o_kpe + ROPE_DIM], w_in[:, o_kpe:o_kpe + half]], axis=1)
    w_in_ext = jnp.concatenate(
        [w_in, swapped, jnp.zeros((d, LANES - 2 * ROPE_DIM), w_in.dtype)], axis=1).astype(BF16)
    wq = w_q_up.reshape(q_rank, N_HEADS_B, NOPE_DIM + ROPE_DIM)
    wq_rope = wq[:, :, NOPE_DIM:]
    wq_rope_sw = jnp.concatenate([wq_rope[:, :, half:], wq_rope[:, :, :half]], axis=2)
    w_q_perm = jnp.concatenate([
        wq[:, :, :NOPE_DIM].reshape(q_rank, -1),
        wq_rope.reshape(q_rank, -1),
        wq_rope_sw.reshape(q_rank, -1)], axis=1).astype(BF16)
    w_kv = w_kv_up.reshape(kv_rank, N_HEADS_B, NOPE_DIM + V_DIM)
    w_uk_t = jnp.transpose(w_kv[..., :NOPE_DIM], (1, 2, 0))
    w_uv = jnp.transpose(w_kv[..., NOPE_DIM:], (1, 0, 2))
    odd = (jnp.arange(N_HEADS_B) % 2 == 1)
    zk = jnp.zeros_like(w_uk_t)
    w_uk_pad = jnp.where(odd[:, None, None],
                         jnp.concatenate([zk, w_uk_t], axis=1),
                         jnp.concatenate([w_uk_t, zk], axis=1)).astype(BF16)
    zv = jnp.zeros_like(w_uv)
    w_uv_pad = jnp.where(odd[:, None, None],
                         jnp.concatenate([zv, w_uv], axis=2),
                         jnp.concatenate([w_uv, zv], axis=2)).astype(BF16)
    return w_in_ext, w_q_perm, w_uk_pad, w_uv_pad


def _prep_spatial(w_sp, b_sp, n):
    wm = jnp.tril(w_sp[:, :n, :n])
    reps = CHUNK // n
    eye = jnp.eye(reps, dtype=w_sp.dtype)
    big = jnp.einsum('ab,gts->gatbs', eye, wm).reshape(G_D, CHUNK, CHUNK)
    w_cat = jnp.concatenate([big[0::2], big[1::2]], axis=2).astype(BF16)
    b_rows = jnp.tile(b_sp[:, :n].T, (reps, 1))
    return w_cat, b_rows


def _trunk(x, mods, pos, hist_a, hist_c, attend, chunk_rows, with_v, ql_dtype, p):
    b, s, d = x.shape
    depth = p['w_ffn_gate'].shape[0]
    cosq, sinq, ropek = _rope_tables(pos)
    conv_a, ckvs, kpes, conv_c, vds = [], [], [], [], []
    for l in range(depth):
        mod3 = mods[l].reshape(b, 1, -1)
        i = l // 2
        if l % 2 == 0:
            w_in_ext, w_q_perm, w_uk_pad, w_uv_pad = p['ab'][i]
            ya, q_cat, ckv, kpe, k_cat, nconv = _ab_in(
                x, mod3, p['norm_mix_pre'][l], w_in_ext, p['conv_a_w'][i], hist_a[i], p['q_norm'][i],
                w_q_perm, p['kv_norm'][i], w_uk_pad, cosq, sinq, ropek, ql_dtype)
            y1 = attend(i, q_cat, ckv, kpe, k_cat, w_uv_pad)
            y0 = ya
            w_out = p['w_out_ab'][i]
            conv_a.append(nconv)
            ckvs.append(ckv.reshape(b, s, -1))
            kpes.append(kpe.reshape(b, s, -1))
        else:
            w_sp_cat, b_rows = _prep_spatial(p['w_spatial'][i], p['b_spatial'][i], chunk_rows)
            d_c = p['conv_c_w'].shape[-1]
            b_sp_rows = jnp.repeat(b_rows, d_c // G_D, axis=1)
            outs = _cd_in(x, mod3, p['norm_mix_pre'][l], p['w_in_cd'][i], p['conv_c_w'][i], p['conv_c_b'][i],
                          p['ln_c_g'][i], p['ln_c_b'][i], p['ln_v_g'][i], p['ln_v_b'][i], hist_c[i],
                          w_sp_cat, b_sp_rows, with_v)
            y0, y1, nconv = outs[:3]
            if with_v:
                vds.append(outs[3].reshape(b, s, -1))
            w_out = p['w_out_cd'][i]
            conv_c.append(nconv)
        x = _out_ffn(y0, y1, x, mod3, p['norm_mix_post'][l], p['norm_ffn_pre'][l], p['norm_ffn_post'][l],
                     w_out, p['w_ffn_gate'], p['w_ffn_up'], p['w_ffn_down'], l)
    return x, conv_a, ckvs, kpes, conv_c, vds


def kernel(x_prompt, x_sample, cache_ckv, cache_kpe, state_conv_a, state_conv_c, page_table, c_prompt, c_sample, w_mod, b_mod, norm_mix_pre, norm_mix_post, norm_ffn_pre, norm_ffn_post, w_in_ab, conv_a_w, q_norm, w_q_up, kv_norm, w_kv_up, w_out_ab, w_in_cd, conv_c_w, conv_c_b, ln_c_g, ln_c_b, ln_v_g, ln_v_b, w_spatial, b_spatial, w_out_cd, w_ffn_gate, w_ffn_up, w_ffn_down):
    b_p, s_p, d = x_prompt.shape
    b_s, s_s, _ = x_sample.shape
    n_ab, n_cd = w_in_ab.shape[0], w_in_cd.shape[0]
    d_a = conv_a_w.shape[-1]
    d_c = conv_c_w.shape[-1]
    q_rank, kv_rank = q_norm.shape[-1], kv_norm.shape[-1]

    p = {
        'norm_mix_pre': norm_mix_pre, 'norm_mix_post': norm_mix_post, 'norm_ffn_pre': norm_ffn_pre,
        'norm_ffn_post': norm_ffn_post, 'conv_a_w': conv_a_w, 'q_norm': q_norm, 'kv_norm': kv_norm,
        'w_out_ab': w_out_ab.astype(BF16), 'w_in_cd': w_in_cd.astype(BF16), 'conv_c_w': conv_c_w,
        'conv_c_b': conv_c_b, 'ln_c_g': ln_c_g, 'ln_c_b': ln_c_b, 'ln_v_g': ln_v_g, 'ln_v_b': ln_v_b,
        'w_spatial': w_spatial, 'b_spatial': b_spatial, 'w_out_cd': w_out_cd.astype(BF16),
        'w_ffn_gate': w_ffn_gate.astype(BF16), 'w_ffn_up': w_ffn_up.astype(BF16),
        'w_ffn_down': w_ffn_down.astype(BF16),
        'ab': [_prep_ab_weights(w_in_ab[i], w_q_up[i], w_kv_up[i], d_a, q_rank, kv_rank) for i in range(n_ab)],
    }

    mods = _modulation(jnp.concatenate([c_prompt, c_sample], axis=0), w_mod, b_mod)
    mods_p, mods_s = mods[:, :b_p], mods[:, b_p:]

    zeros_a = jnp.zeros((n_ab, b_p, state_conv_a.shape[2], d_a), F32)
    zeros_c = jnp.zeros((n_cd, b_p, state_conv_c.shape[2], d_c), F32)

    def attend_prompt(i, q_cat, ckv, kpe, k_cat, w_uv_pad):
        return _prompt_attention(q_cat, k_cat, w_uv_pad, b_p, s_p)

    y_p, sa_p, ckv_p, kpe_p, sc_p, _ = _trunk(
        x_prompt, mods_p, jnp.arange(s_p), zeros_a, zeros_c, attend_prompt, CHUNK, False, BF16, p)

    past_len = page_table.shape[1] * PAGE_SIZE

    cache_kpe_t = jnp.swapaxes(cache_kpe, 2, 3)

    def attend_sample(i, q_cat, ckv, kpe, k_cat, w_uv_pad):
        return _paged_attention(q_cat, ckv, kpe, cache_ckv, cache_kpe_t, page_table, i, w_uv_pad, s_s)

    y_s, sa_s, ckv_s, kpe_s, sc_s, vd_s = _trunk(
        x_sample, mods_s, past_len + jnp.arange(s_s), state_conv_a, state_conv_c, attend_sample, s_s,
        True, F32, p)

    st = jnp.stack
    return (y_p, y_s, st(sa_p), st(ckv_p), st(kpe_p), st(sc_p), st(sa_s), st(ckv_s), st(kpe_s), st(sc_s),
            st(vd_s))
```

```python
import functools

import jax
import jax.numpy as jnp
from jax import lax
from jax.experimental import pallas as pl
from jax.experimental.pallas import tpu as pltpu

N_HEADS_B = 8
NOPE_DIM = 64
ROPE_DIM = 32
V_DIM = 64
ROPE_THETA = 10000.0
G_D = 8
CHUNK = 128
PAGE_SIZE = 128
EPS = 1e-6
NEG_INF = -1e30
SCALE_B = (NOPE_DIM + ROPE_DIM) ** -0.5

LANES = 128
SUBLANES = 8
VMEM_LIMIT = 56 * 1024 * 1024

ROW_TILE = 512
ATTN_Q_TILE = 256
PAGES_PER_CHUNK = 32
PAGED_SLOTS = 4
PAGED_SEQS_PER_STEP = 4

F32 = jnp.float32
BF16 = jnp.bfloat16


def _rms(x, g):
    return x * lax.rsqrt(jnp.mean(x * x, axis=-1, keepdims=True) + EPS) * g


def _ln(x, g, b):
    xc = x - jnp.mean(x, axis=-1, keepdims=True)
    return xc * lax.rsqrt(jnp.mean(xc * xc, axis=-1, keepdims=True) + EPS) * g + b


def _silu(x):
    return x * jax.nn.sigmoid(x)


def _dot(a, b):
    return jnp.dot(a.astype(BF16), b.astype(BF16), preferred_element_type=F32)


def _dot_nt(a, b):
    return lax.dot_general(a.astype(BF16), b.astype(BF16), (((1,), (1,)), ((), ())),
                           preferred_element_type=F32)


def _params(*sem):
    return pltpu.CompilerParams(dimension_semantics=sem, vmem_limit_bytes=VMEM_LIMIT)


def _mod_kernel(c_ref, w_ref, b_ref, o_ref):
    o_ref[0] = _dot(_silu(c_ref[...]), w_ref[0]) + b_ref[0]


def _modulation(c_all, w_mod, b_mod):
    n_l, d, n = w_mod.shape
    bc = c_all.shape[0]
    tn = n // 4
    return pl.pallas_call(
        _mod_kernel,
        grid=(n_l, n // tn),
        in_specs=[pl.BlockSpec((bc, d), lambda l, j: (0, 0)),
                  pl.BlockSpec((1, d, tn), lambda l, j: (l, 0, j)),
                  pl.BlockSpec((1, 1, tn), lambda l, j: (l, 0, j))],
        out_specs=pl.BlockSpec((1, bc, tn), lambda l, j: (l, 0, j)),
        out_shape=jax.ShapeDtypeStruct((n_l, bc, n), F32),
        compiler_params=_params("arbitrary", "arbitrary"),
    )(c_all, w_mod, b_mod.reshape(n_l, 1, n))


def _mod_spec(bt, d, k, grid_rank):
    if grid_rank == 2:
        return pl.BlockSpec((bt, 1, d), lambda b, s: (b, 0, k))
    return pl.BlockSpec((bt, 1, d), lambda b, s, f: (b, 0, k))


def _const_spec(shape, grid_rank):
    zeros = (0,) * len(shape)
    if grid_rank == 2:
        return pl.BlockSpec(shape, lambda b, s: zeros)
    return pl.BlockSpec(shape, lambda b, s, f: zeros)


def _tiles(b, s):
    if s >= ROW_TILE:
        assert s % ROW_TILE == 0
        return 1, ROW_TILE
    assert ROW_TILE % s == 0 and s % SUBLANES == 0 and b % (ROW_TILE // s) == 0
    return ROW_TILE // s, s


def _ab_in_kernel(x_ref, sh_ref, sc_ref, g_ref, w_ref, cw_ref, hist_ref, qn_ref, wq_ref, kvn_ref,
                  wuk_ref, cosq_ref, sinq_ref, ropek_ref,
                  ya_ref, qcat_ref, ckv_ref, kpe_ref, kcat_ref, nconv_ref,
                  zs_ref, *, bt, ts, d_a, q_rank, kv_rank):
    tm = bt * ts
    hist_rows = nconv_ref.shape[1]
    x = x_ref[...]
    h = _rms(x, g_ref[...]) * (1.0 + sc_ref[...]) + sh_ref[...]
    proj = _dot(h.reshape(tm, h.shape[-1]), w_ref[...])
    o_cq = 3 * d_a
    o_ckv = o_cq + q_rank
    o_kpe = o_ckv + kv_rank
    gate_out = proj[:, 0:d_a].reshape(bt, ts, d_a)
    z = (proj[:, d_a:2 * d_a] * proj[:, 2 * d_a:3 * d_a]).reshape(bt, ts, d_a)

    base = SUBLANES - hist_rows

    @pl.when(pl.program_id(1) == 0)
    def _():
        zs_ref[:, base:SUBLANES, :] = hist_ref[...]

    zs_ref[:, SUBLANES:SUBLANES + ts, :] = z
    cw = cw_ref[...]
    conv = z * cw[hist_rows:hist_rows + 1, :]
    for k in range(hist_rows):
        conv = conv + zs_ref[:, base + k:base + k + ts, :] * cw[k:k + 1, :]
    ya_ref[...] = (gate_out * conv).reshape(tm, d_a).astype(ya_ref.dtype)
    tail = zs_ref[:, ts + base:ts + SUBLANES, :]
    nconv_ref[...] = tail
    zs_ref[:, base:SUBLANES, :] = tail

    cqn = _rms(proj[:, o_cq:o_ckv], qn_ref[...])
    q = _dot(cqn, wq_ref[...]) * SCALE_B
    n_nope = N_HEADS_B * NOPE_DIM
    n_rope = N_HEADS_B * ROPE_DIM
    qr = q[:, n_nope:n_nope + n_rope].reshape(bt, ts, n_rope)
    qs = q[:, n_nope + n_rope:n_nope + 2 * n_rope].reshape(bt, ts, n_rope)
    qpe = (qr * cosq_ref[...][None] + qs * sinq_ref[...][None]).reshape(tm, n_rope)
    for hh in range(N_HEADS_B):
        pair = q[:, LANES * (hh // 2):LANES * (hh // 2 + 1)]
        q_lat = _dot(pair, wuk_ref[hh])
        qcat_ref[hh] = jnp.concatenate(
            [q_lat, qpe[:, hh * ROPE_DIM:(hh + 1) * ROPE_DIM]], axis=-1).astype(qcat_ref.dtype)

    ckvn = _rms(proj[:, o_ckv:o_kpe], kvn_ref[...])
    ckv_ref[...] = ckvn
    kt = proj[:, o_kpe:o_kpe + LANES].reshape(bt, ts, LANES) * ropek_ref[...][None]
    kpe = (kt[:, :, 0:ROPE_DIM] + kt[:, :, ROPE_DIM:2 * ROPE_DIM]).reshape(tm, ROPE_DIM)
    kpe_ref[...] = kpe
    kcat_ref[...] = jnp.concatenate([ckvn, kpe], axis=-1).astype(BF16)


def _ab_in(x, mod3, g_pre, w_in_ext, conv_w, hist, q_norm, w_q_perm, kv_norm, w_uk_pad,
           cosq, sinq, ropek, ql_dtype):
    b, s, d = x.shape
    m = b * s
    bt, ts = _tiles(b, s)
    tm = bt * ts
    d_a = conv_w.shape[1]
    q_rank = q_norm.shape[-1]
    kv_rank = kv_norm.shape[-1]
    hist_rows = hist.shape[1]
    n_ext = w_in_ext.shape[1]
    grid = (b // bt, s // ts)
    row = lambda bb, ss: (bb * (s // ts) + ss, 0)
    kern = functools.partial(_ab_in_kernel, bt=bt, ts=ts, d_a=d_a, q_rank=q_rank, kv_rank=kv_rank)
    d_qk = kv_rank + ROPE_DIM
    out_shape = (
        jax.ShapeDtypeStruct((m, d_a), BF16),
        jax.ShapeDtypeStruct((N_HEADS_B, m, d_qk), ql_dtype),
        jax.ShapeDtypeStruct((m, kv_rank), F32),
        jax.ShapeDtypeStruct((m, ROPE_DIM), F32),
        jax.ShapeDtypeStruct((m, d_qk), BF16),
        jax.ShapeDtypeStruct((b, hist_rows, d_a), F32),
    )
    out_specs = (
        pl.BlockSpec((tm, d_a), row),
        pl.BlockSpec((N_HEADS_B, tm, d_qk), lambda bb, ss: (0, bb * (s // ts) + ss, 0)),
        pl.BlockSpec((tm, kv_rank), row),
        pl.BlockSpec((tm, ROPE_DIM), row),
        pl.BlockSpec((tm, d_qk), row),
        pl.BlockSpec((bt, hist_rows, d_a), lambda bb, ss: (bb, 0, 0)),
    )
    in_specs = [
        pl.BlockSpec((bt, ts, d), lambda bb, ss: (bb, ss, 0)),
        _mod_spec(bt, d, 0, 2), _mod_spec(bt, d, 1, 2),
        _const_spec((1, d), 2),
        _const_spec((d, n_ext), 2),
        _const_spec(conv_w.shape, 2),
        pl.BlockSpec((bt, hist_rows, d_a), lambda bb, ss: (bb, 0, 0)),
        _const_spec((1, q_rank), 2),
        _const_spec(w_q_perm.shape, 2),
        _const_spec((1, kv_rank), 2),
        _const_spec(w_uk_pad.shape, 2),
        pl.BlockSpec((ts, cosq.shape[1]), lambda bb, ss: (ss, 0)),
        pl.BlockSpec((ts, sinq.shape[1]), lambda bb, ss: (ss, 0)),
        pl.BlockSpec((ts, LANES), lambda bb, ss: (ss, 0)),
    ]
    return pl.pallas_call(
        kern, grid=grid, in_specs=in_specs, out_specs=out_specs, out_shape=out_shape,
        scratch_shapes=[pltpu.VMEM((bt, ts + SUBLANES, d_a), F32)],
        compiler_params=_params("arbitrary", "arbitrary"),
    )(x, mod3, mod3, g_pre.reshape(1, d), w_in_ext, conv_w, hist, q_norm.reshape(1, q_rank),
      w_q_perm, kv_norm.reshape(1, kv_rank), w_uk_pad, cosq, sinq, ropek)


def _fold_lanes(x, op):
    out = x[:, 0:LANES]
    for t in range(1, x.shape[1] // LANES):
        out = op(out, x[:, t * LANES:(t + 1) * LANES])
    return out


def _value_up(o, wuv_ref, rows):
    parts = []
    for j in range(N_HEADS_B // 2):
        o0 = o[(2 * j) * rows:(2 * j + 1) * rows]
        o1 = o[(2 * j + 1) * rows:(2 * j + 2) * rows]
        parts.append(_dot(o0, wuv_ref[2 * j]) + _dot(o1, wuv_ref[2 * j + 1]))
    return jnp.concatenate(parts, axis=-1)


def _prompt_attn_kernel(q_ref, k_ref, wuv_ref, yb_ref, s_ref, stat_ref, lsum_ref, acc_ref, *, tq, r):
    i = pl.program_id(1)
    rows = N_HEADS_B * tq
    q = q_ref[...].reshape(rows, q_ref.shape[-1])

    def key_block(j):
        return k_ref[pl.ds(pl.multiple_of(j * tq, tq), tq), :]

    def scores_pass(j, mp):
        s = _dot_nt(q, key_block(j))
        s_ref[j] = s
        return jnp.maximum(mp, _fold_lanes(s, jnp.maximum))

    stat_ref[...] = jnp.full(stat_ref.shape, NEG_INF, F32)

    def scores_body(j, carry):
        stat_ref[...] = scores_pass(j, stat_ref[...])
        return carry

    lax.fori_loop(0, i, scores_body, 0)
    s = _dot_nt(q, key_block(i))
    qpos = lax.broadcasted_iota(jnp.int32, s.shape, 0) & (tq - 1)
    kpos = lax.broadcasted_iota(jnp.int32, s.shape, 1)
    s = jnp.where(kpos <= qpos, s, NEG_INF)
    s_ref[i] = s
    mp = jnp.maximum(stat_ref[...], _fold_lanes(s, jnp.maximum))
    m = jnp.max(mp, axis=-1, keepdims=True)
    stat_ref[...] = jnp.broadcast_to(m, stat_ref.shape)
    acc_ref[...] = jnp.zeros(acc_ref.shape, F32)

    lsum_ref[...] = jnp.zeros(lsum_ref.shape, F32)

    def value_body(j, carry):
        mrow = stat_ref[...]
        p = jnp.exp(s_ref[j] - jnp.concatenate([mrow] * (tq // LANES), axis=-1))
        acc_ref[...] += jnp.dot(p.astype(BF16), key_block(j)[:, 0:r], preferred_element_type=F32)
        lsum_ref[...] += _fold_lanes(p, jnp.add)
        return carry

    lax.fori_loop(0, i + 1, value_body, 0)
    o = acc_ref[...] / jnp.sum(lsum_ref[...], axis=-1, keepdims=True)
    yb_ref[...] = _value_up(o, wuv_ref, tq).astype(yb_ref.dtype)


def _prompt_attention(q_cat, k_cat, w_uv_pad, b, s):
    n_h, m, d_qk = q_cat.shape
    r = d_qk - ROPE_DIM
    tq = ATTN_Q_TILE
    nq = s // tq
    rows = n_h * tq
    kern = functools.partial(_prompt_attn_kernel, tq=tq, r=r)
    return pl.pallas_call(
        kern, grid=(b, nq),
        in_specs=[pl.BlockSpec((n_h, tq, d_qk), lambda bb, i: (0, bb * nq + i, 0)),
                  pl.BlockSpec((s, d_qk), lambda bb, i: (bb, 0)),
                  _const_spec(w_uv_pad.shape, 2)],
        out_specs=pl.BlockSpec((tq, n_h * V_DIM), lambda bb, i: (bb * nq + i, 0)),
        out_shape=jax.ShapeDtypeStruct((m, n_h * V_DIM), BF16),
        scratch_shapes=[pltpu.VMEM((nq, rows, tq), F32), pltpu.VMEM((rows, LANES), F32),
                        pltpu.VMEM((rows, LANES), F32), pltpu.VMEM((rows, r), F32)],
        compiler_params=_params("arbitrary", "arbitrary"),
    )(q_cat, k_cat, w_uv_pad)


def _paged_attn_kernel(pt_ref, q_ref, cnew_ref, knew_ref, wuv_ref, ckv_hbm, kpet_hbm, yb_ref,
                       raw_k, raw_p, kb_ref, s_ref, sems, *, layer, n_chunks, n_slots, pages_chunk, page,
                       s_new, r, seqs_step):
    t = pl.program_id(0)
    rows = N_HEADS_B * s_new
    total = seqs_step * n_chunks
    q_lats, q_pes = [], []
    for g in range(seqs_step):
        qc = q_ref[:, g * s_new:(g + 1) * s_new, :].reshape(rows, q_ref.shape[-1])
        q_lats.append(qc[:, 0:r].astype(BF16))
        q_pes.append(qc[:, r:].astype(BF16))

    def chunk_copies(n, slot):
        seq = t * seqs_step + n // n_chunks
        c = n % n_chunks
        return chunk_copies_of(seq, c, slot)

    def chunk_copies_of(seq, c, slot):
        copies = []
        for i in range(pages_chunk):
            pg = pt_ref[seq, c * pages_chunk + i]
            copies.append(pltpu.make_async_copy(
                ckv_hbm.at[layer, pg], raw_k.at[slot, pl.ds(i * page, page)], sems.at[0, slot]))
            copies.append(pltpu.make_async_copy(
                kpet_hbm.at[layer, pg], raw_p.at[slot, :, pl.ds(i * page, page)], sems.at[1, slot]))
        return copies

    def start_chunk(n, slot):
        for i, cp in enumerate(chunk_copies(n, slot)):
            cp.start(priority=(i // 2) % 2)

    ahead = n_slots - 1

    @pl.when(t == 0)
    def _():
        for n in range(ahead):
            start_chunk(n, n % n_slots)

    parts = [[] for _ in range(seqs_step)]

    def finish(n):
        s = s_ref[n % 2]
        m_c = jnp.max(_fold_lanes(s, jnp.maximum), axis=-1, keepdims=True)
        p = jnp.exp(s - m_c)
        l_c = jnp.sum(_fold_lanes(p, jnp.add), axis=-1, keepdims=True)
        o_c = jnp.dot(p.astype(BF16), kb_ref[n % 2], preferred_element_type=F32)
        g = n // n_chunks
        parts[g].append((m_c, l_c, o_c))
        if n % n_chunks == n_chunks - 1:
            merge(g)

    def merge(g):
        tok = slice(g * s_new, (g + 1) * s_new)
        k_new = cnew_ref[tok, :].astype(BF16)
        s_n = _dot_nt(q_lats[g], k_new) + _dot_nt(q_pes[g], knew_ref[tok, :])
        qpos = lax.broadcasted_iota(jnp.int32, s_n.shape, 0) & (s_new - 1)
        kpos = lax.broadcasted_iota(jnp.int32, s_n.shape, 1)
        s_n = jnp.where(kpos <= qpos, s_n, NEG_INF)
        m_n = jnp.max(s_n, axis=-1, keepdims=True)
        p_n = jnp.exp(s_n - m_n)
        parts[g].append((m_n, jnp.sum(p_n, axis=-1, keepdims=True),
                         jnp.dot(p_n.astype(BF16), k_new, preferred_element_type=F32)))
        m = parts[g][0][0]
        for m_c, _, _ in parts[g][1:]:
            m = jnp.maximum(m, m_c)
        l = jnp.zeros((rows, 1), F32)
        acc = jnp.zeros((rows, r), F32)
        for m_c, l_c, o_c in parts[g]:
            w = jnp.exp(m_c - m)
            l = l + w * l_c
            acc = acc + w * o_c
        yb_ref[tok, :] = _value_up(acc / l, wuv_ref, s_new).astype(yb_ref.dtype)

    for n in range(total):
        slot = n % n_slots
        pltpu.make_async_copy(raw_k.at[slot], raw_k.at[slot], sems.at[0, slot]).wait()
        pltpu.make_async_copy(raw_p.at[slot], raw_p.at[slot], sems.at[1, slot]).wait()
        nxt = n + ahead
        if nxt < total:
            start_chunk(nxt, nxt % n_slots)
        else:
            @pl.when(t + 1 < pl.num_programs(0))
            def _():
                start_chunk(nxt, nxt % n_slots)
        k = raw_k[slot].astype(BF16)
        kb_ref[n % 2] = k
        g = n // n_chunks
        s_ref[n % 2] = (_dot_nt(q_lats[g], k)
                        + jnp.dot(q_pes[g], raw_p[slot].astype(BF16), preferred_element_type=F32))
        if n:
            finish(n - 1)
    finish(total - 1)


def _paged_attention(q_cat, ckv_new, kpe_new, cache_ckv, cache_kpe_t, page_table, layer, w_uv_pad, s_new):
    n_h, m, d_qk = q_cat.shape
    r = d_qk - ROPE_DIM
    b, n_pages = page_table.shape
    page = cache_ckv.shape[2]
    pages_chunk = min(PAGES_PER_CHUNK, n_pages)
    n_chunks = n_pages // pages_chunk
    n_slots = PAGED_SLOTS
    g = PAGED_SEQS_PER_STEP
    assert s_new & (s_new - 1) == 0 and n_pages % pages_chunk == 0
    assert b % g == 0 and (g * n_chunks) % n_slots == 0
    keys_chunk = pages_chunk * page
    rows = n_h * s_new
    kern = functools.partial(_paged_attn_kernel, layer=layer, n_chunks=n_chunks, n_slots=n_slots,
                             pages_chunk=pages_chunk, page=page, s_new=s_new, r=r, seqs_step=g)
    grid_spec = pltpu.PrefetchScalarGridSpec(
        num_scalar_prefetch=1, grid=(b // g,),
        in_specs=[pl.BlockSpec((n_h, g * s_new, d_qk), lambda bb, pt: (0, bb, 0)),
                  pl.BlockSpec((g * s_new, r), lambda bb, pt: (bb, 0)),
                  pl.BlockSpec((g * s_new, ROPE_DIM), lambda bb, pt: (bb, 0)),
                  pl.BlockSpec(w_uv_pad.shape, lambda bb, pt: (0, 0, 0)),
                  pl.BlockSpec(memory_space=pl.ANY),
                  pl.BlockSpec(memory_space=pl.ANY)],
        out_specs=pl.BlockSpec((g * s_new, n_h * V_DIM), lambda bb, pt: (bb, 0)),
        scratch_shapes=[pltpu.VMEM((n_slots, keys_chunk, r), F32),
                        pltpu.VMEM((n_slots, ROPE_DIM, keys_chunk), F32),
                        pltpu.VMEM((2, keys_chunk, r), BF16),
                        pltpu.VMEM((2, rows, keys_chunk), F32),
                        pltpu.SemaphoreType.DMA((2, n_slots))])
    return pl.pallas_call(
        kern, grid_spec=grid_spec,
        out_shape=jax.ShapeDtypeStruct((m, n_h * V_DIM), F32),
        compiler_params=_params("arbitrary"),
    )(page_table, q_cat, ckv_new, kpe_new, w_uv_pad, cache_ckv, cache_kpe_t)


def _cd_in_kernel(x_ref, sh_ref, sc_ref, g_ref, w_ref, cw_ref, cb_ref, lcg_ref, lcb_ref, lvg_ref, lvb_ref,
                  hist_ref, wsp_ref, bsp_ref, *rest, bt, ts, d_c, with_v):
    if with_v:
        yc_ref, yd_ref, nconv_ref, vn_ref, hs_ref = rest
    else:
        yc_ref, yd_ref, nconv_ref, hs_ref = rest
    tm = bt * ts
    hist_rows = nconv_ref.shape[1]
    n_taps = hist_rows + 1
    pad = hs_ref.shape[1] - ts
    base = pad - hist_rows
    x = x_ref[...]
    h = _rms(x, g_ref[...]) * (1.0 + sc_ref[...]) + sh_ref[...]
    proj = _dot(h.reshape(tm, h.shape[-1]), w_ref[...])

    glu = (proj[:, 0:d_c] * jax.nn.sigmoid(proj[:, d_c:2 * d_c])).reshape(bt, ts, d_c)

    @pl.when(pl.program_id(1) == 0)
    def _():
        if base:
            hs_ref[:, 0:base, :] = jnp.zeros((bt, base, d_c), F32)
        hs_ref[:, base:pad, :] = hist_ref[...]

    hs_ref[:, pad:pad + ts, :] = glu
    cw = cw_ref[...]
    conv = glu * cw[n_taps - 1:n_taps, :] + cb_ref[...]
    for rem in range(SUBLANES):
        part = None
        for row0 in range(0, pad, SUBLANES):
            k = row0 + rem - base
            if 0 <= k < n_taps - 1:
                term = hs_ref[:, row0:row0 + ts + SUBLANES, :] * cw[k:k + 1, :]
                part = term if part is None else part + term
        if part is not None:
            conv = conv + part[:, rem:rem + ts, :]
    yc = _silu(_ln(conv, lcg_ref[...], lcb_ref[...]))
    yc_ref[...] = yc.reshape(tm, d_c).astype(yc_ref.dtype)
    tail = hs_ref[:, ts + base:ts + pad, :]
    nconv_ref[...] = tail
    hs_ref[:, base:pad, :] = tail

    u = proj[:, 2 * d_c:3 * d_c]
    vn = _ln(proj[:, 3 * d_c:4 * d_c], lvg_ref[...], lvb_ref[...])
    if with_v:
        vn_ref[...] = vn
    low = lax.broadcasted_iota(jnp.int32, (CHUNK, LANES), 1) < (LANES // 2)
    bsp = bsp_ref[...]
    for c in range(tm // CHUNK):
        parts = []
        for j in range(d_c // LANES):
            vj = vn[c * CHUNK:(c + 1) * CHUNK, j * LANES:(j + 1) * LANES]
            rhs = jnp.concatenate([jnp.where(low, vj, 0.0), jnp.where(low, 0.0, vj)], axis=0)
            parts.append(_dot(wsp_ref[j], rhs))
        sv = jnp.concatenate(parts, axis=-1) + bsp
        yd_ref[c * CHUNK:(c + 1) * CHUNK, :] = (u[c * CHUNK:(c + 1) * CHUNK] * sv).astype(yd_ref.dtype)


def _cd_in(x, mod3, g_pre, w_in, conv_w, conv_b, ln_c_g, ln_c_b, ln_v_g, ln_v_b, hist, w_sp_cat, b_sp_rows,
           with_v):
    b, s, d = x.shape
    m = b * s
    bt, ts = _tiles(b, s)
    tm = bt * ts
    d_c = conv_w.shape[1]
    hist_rows = hist.shape[1]
    pad = -(-hist_rows // SUBLANES) * SUBLANES
    grid = (b // bt, s // ts)
    row = lambda bb, ss: (bb * (s // ts) + ss, 0)
    vec = lambda a: a.reshape(1, -1)
    kern = functools.partial(_cd_in_kernel, bt=bt, ts=ts, d_c=d_c, with_v=with_v)
    out_shape = [jax.ShapeDtypeStruct((m, d_c), BF16), jax.ShapeDtypeStruct((m, d_c), BF16),
                 jax.ShapeDtypeStruct((b, hist_rows, d_c), F32)]
    out_specs = [pl.BlockSpec((tm, d_c), row), pl.BlockSpec((tm, d_c), row),
                 pl.BlockSpec((bt, hist_rows, d_c), lambda bb, ss: (bb, 0, 0))]
    if with_v:
        out_shape.append(jax.ShapeDtypeStruct((m, d_c), F32))
        out_specs.append(pl.BlockSpec((tm, d_c), row))
    in_specs = [
        pl.BlockSpec((bt, ts, d), lambda bb, ss: (bb, ss, 0)),
        _mod_spec(bt, d, 0, 2), _mod_spec(bt, d, 1, 2),
        _const_spec((1, d), 2),
        _const_spec(w_in.shape, 2),
        _const_spec(conv_w.shape, 2),
        _const_spec((1, d_c), 2), _const_spec((1, d_c), 2), _const_spec((1, d_c), 2),
        _const_spec((1, d_c), 2), _const_spec((1, d_c), 2),
        pl.BlockSpec((bt, hist_rows, d_c), lambda bb, ss: (bb, 0, 0)),
        _const_spec(w_sp_cat.shape, 2),
        _const_spec(b_sp_rows.shape, 2),
    ]
    return pl.pallas_call(
        kern, grid=grid, in_specs=in_specs, out_specs=tuple(out_specs), out_shape=tuple(out_shape),
        scratch_shapes=[pltpu.VMEM((bt, ts + pad, d_c), F32)],
        compiler_params=_params("arbitrary", "arbitrary"),
    )(x, mod3, mod3, vec(g_pre), w_in, conv_w, vec(conv_b), vec(ln_c_g), vec(ln_c_b), vec(ln_v_g),
      vec(ln_v_b), hist, w_sp_cat, b_sp_rows)


def _out_ffn_kernel(y0_ref, y1_ref, x_ref, gm_ref, shf_ref, scf_ref, gf_ref, npost_ref, nfpre_ref, nfpost_ref,
                    wo_ref, wg_ref, wu_ref, wd_ref, o_ref, x1_ref, h_ref, acc_ref, *, bt, ts):
    f = pl.program_id(2)
    tm = bt * ts
    d = x_ref.shape[-1]
    k0 = y0_ref.shape[-1]

    @pl.when(f == 0)
    def _():
        y = _dot(y0_ref[...], wo_ref[0:k0, :]) + _dot(y1_ref[...], wo_ref[k0:, :])
        x1 = x_ref[...] + gm_ref[...] * _rms(y, npost_ref[...]).reshape(bt, ts, d)
        x1_ref[...] = x1
        h = _rms(x1, nfpre_ref[...]) * (1.0 + scf_ref[...]) + shf_ref[...]
        h_ref[...] = h.reshape(tm, d).astype(BF16)
        acc_ref[...] = jnp.zeros(acc_ref.shape, F32)

    h = h_ref[...]
    a = _silu(_dot(h, wg_ref[...])) * _dot(h, wu_ref[...])
    acc_ref[...] += _dot(a, wd_ref[...])

    @pl.when(f == pl.num_programs(2) - 1)
    def _():
        o_ref[...] = x1_ref[...] + gf_ref[...] * _rms(acc_ref[...], nfpost_ref[...]).reshape(bt, ts, d)


def _out_ffn(y0, y1, x, mod3, norm_post, norm_ffn_pre, norm_ffn_post, w_out, w_gate, w_up, w_down, layer):
    b, s, d = x.shape
    bt, ts = _tiles(b, s)
    tm = bt * ts
    d_ff = w_gate.shape[2]
    n_f = 2
    tf = d_ff // n_f
    assert tf % LANES == 0
    grid = (b // bt, s // ts, n_f)
    row = lambda bb, ss, f: (bb * (s // ts) + ss, 0)
    vec = lambda a: a.reshape(1, -1)
    kern = functools.partial(_out_ffn_kernel, bt=bt, ts=ts)
    in_specs = [
        pl.BlockSpec((tm, y0.shape[1]), row),
        pl.BlockSpec((tm, y1.shape[1]), row),
        pl.BlockSpec((bt, ts, d), lambda bb, ss, f: (bb, ss, 0)),
        _mod_spec(bt, d, 2, 3), _mod_spec(bt, d, 3, 3), _mod_spec(bt, d, 4, 3), _mod_spec(bt, d, 5, 3),
        _const_spec((1, d), 3), _const_spec((1, d), 3), _const_spec((1, d), 3),
        _const_spec(w_out.shape, 3),
        pl.BlockSpec((None, d, tf), lambda bb, ss, f: (layer, 0, f)),
        pl.BlockSpec((None, d, tf), lambda bb, ss, f: (layer, 0, f)),
        pl.BlockSpec((None, tf, d), lambda bb, ss, f: (layer, f, 0)),
    ]
    return pl.pallas_call(
        kern, grid=grid, in_specs=in_specs,
        out_specs=pl.BlockSpec((bt, ts, d), lambda bb, ss, f: (bb, ss, 0)),
        out_shape=jax.ShapeDtypeStruct((b, s, d), F32),
        scratch_shapes=[pltpu.VMEM((bt, ts, d), F32), pltpu.VMEM((tm, d), BF16), pltpu.VMEM((tm, d), F32)],
        compiler_params=_params("arbitrary", "arbitrary", "arbitrary"),
    )(y0, y1, x, mod3, mod3, mod3, mod3, vec(norm_post), vec(norm_ffn_pre), vec(norm_ffn_post),
      w_out, w_gate, w_up, w_down)


def _rope_tables(pos):
    inv_freq = ROPE_THETA ** (-jnp.arange(0, ROPE_DIM, 2, dtype=F32) / ROPE_DIM)
    ang = pos.astype(F32)[:, None] * inv_freq[None, :]
    cos, sin = jnp.cos(ang), jnp.sin(ang)
    cosq = jnp.tile(jnp.concatenate([cos, cos], axis=-1), (1, N_HEADS_B))
    sinq = jnp.tile(jnp.concatenate([-sin, sin], axis=-1), (1, N_HEADS_B))
    zeros = jnp.zeros((pos.shape[0], LANES - 2 * ROPE_DIM), F32)
    ropek = jnp.concatenate([cos, cos, -sin, sin, zeros], axis=-1)
    return cosq, sinq, ropek


def _prep_ab_weights(w_in, w_q_up, w_kv_up, d_a, q_rank, kv_rank):
    d = w_in.shape[0]
    o_kpe = 3 * d_a + q_rank + kv_rank
    half = ROPE_DIM // 2
    swapped = jnp.concatenate([w_in[:, o_kpe + half:o_kpe + ROPE_DIM], w_in[:, o_kpe:o_kpe + half]], axis=1)
    w_in_ext = jnp.concatenate(
        [w_in, swapped, jnp.zeros((d, LANES - 2 * ROPE_DIM), w_in.dtype)], axis=1).astype(BF16)
    wq = w_q_up.reshape(q_rank, N_HEADS_B, NOPE_DIM + ROPE_DIM)
    wq_rope = wq[:, :, NOPE_DIM:]
    wq_rope_sw = jnp.concatenate([wq_rope[:, :, half:], wq_rope[:, :, :half]], axis=2)
    w_q_perm = jnp.concatenate([
        wq[:, :, :NOPE_DIM].reshape(q_rank, -1),
        wq_rope.reshape(q_rank, -1),
        wq_rope_sw.reshape(q_rank, -1)], axis=1).astype(BF16)
    w_kv = w_kv_up.reshape(kv_rank, N_HEADS_B, NOPE_DIM + V_DIM)
    w_uk_t = jnp.transpose(w_kv[..., :NOPE_DIM], (1, 2, 0))
    w_uv = jnp.transpose(w_kv[..., NOPE_DIM:], (1, 0, 2))
    odd = (jnp.arange(N_HEADS_B) % 2 == 1)
    zk = jnp.zeros_like(w_uk_t)
    w_uk_pad = jnp.where(odd[:, None, None],
                         jnp.concatenate([zk, w_uk_t], axis=1),
                         jnp.concatenate([w_uk_t, zk], axis=1)).astype(BF16)
    zv = jnp.zeros_like(w_uv)
    w_uv_pad = jnp.where(odd[:, None, None],
                         jnp.concatenate([zv, w_uv], axis=2),
                         jnp.concatenate([w_uv, zv], axis=2)).astype(BF16)
    return w_in_ext, w_q_perm, w_uk_pad, w_uv_pad


def _prep_spatial(w_sp, b_sp, n):
    wm = jnp.tril(w_sp[:, :n, :n])
    reps = CHUNK // n
    eye = jnp.eye(reps, dtype=w_sp.dtype)
    big = jnp.einsum('ab,gts->gatbs', eye, wm).reshape(G_D, CHUNK, CHUNK)
    w_cat = jnp.concatenate([big[0::2], big[1::2]], axis=2).astype(BF16)
    b_rows = jnp.tile(b_sp[:, :n].T, (reps, 1))
    return w_cat, b_rows


def _trunk(x, mods, pos, hist_a, hist_c, attend, chunk_rows, with_v, ql_dtype, p):
    b, s, d = x.shape
    depth = p['w_ffn_gate'].shape[0]
    cosq, sinq, ropek = _rope_tables(pos)
    conv_a, ckvs, kpes, conv_c, vds = [], [], [], [], []
    for l in range(depth):
        mod3 = mods[l].reshape(b, 1, -1)
        i = l // 2
        if l % 2 == 0:
            w_in_ext, w_q_perm, w_uk_pad, w_uv_pad = p['ab'][i]
            ya, q_cat, ckv, kpe, k_cat, nconv = _ab_in(
                x, mod3, p['norm_mix_pre'][l], w_in_ext, p['conv_a_w'][i], hist_a[i], p['q_norm'][i],
                w_q_perm, p['kv_norm'][i], w_uk_pad, cosq, sinq, ropek, ql_dtype)
            y1 = attend(i, q_cat, ckv, kpe, k_cat, w_uv_pad)
            y0 = ya
            w_out = p['w_out_ab'][i]
            conv_a.append(nconv)
            ckvs.append(ckv.reshape(b, s, -1))
            kpes.append(kpe.reshape(b, s, -1))
        else:
            w_sp_cat, b_rows = _prep_spatial(p['w_spatial'][i], p['b_spatial'][i], chunk_rows)
            d_c = p['conv_c_w'].shape[-1]
            b_sp_rows = jnp.repeat(b_rows, d_c // G_D, axis=1)
            outs = _cd_in(x, mod3, p['norm_mix_pre'][l], p['w_in_cd'][i], p['conv_c_w'][i], p['conv_c_b'][i],
                          p['ln_c_g'][i], p['ln_c_b'][i], p['ln_v_g'][i], p['ln_v_b'][i], hist_c[i],
                          w_sp_cat, b_sp_rows, with_v)
            y0, y1, nconv = outs[:3]
            if with_v:
                vds.append(outs[3].reshape(b, s, -1))
            w_out = p['w_out_cd'][i]
            conv_c.append(nconv)
        x = _out_ffn(y0, y1, x, mod3, p['norm_mix_post'][l], p['norm_ffn_pre'][l], p['norm_ffn_post'][l],
                     w_out, p['w_ffn_gate'], p['w_ffn_up'], p['w_ffn_down'], l)
    return x, conv_a, ckvs, kpes, conv_c, vds


def kernel(x_prompt, x_sample, cache_ckv, cache_kpe, state_conv_a, state_conv_c, page_table, c_prompt, c_sample, w_mod, b_mod, norm_mix_pre, norm_mix_post, norm_ffn_pre, norm_ffn_post, w_in_ab, conv_a_w, q_norm, w_q_up, kv_norm, w_kv_up, w_out_ab, w_in_cd, conv_c_w, conv_c_b, ln_c_g, ln_c_b, ln_v_g, ln_v_b, w_spatial, b_spatial, w_out_cd, w_ffn_gate, w_ffn_up, w_ffn_down):
    b_p, s_p, d = x_prompt.shape
    b_s, s_s, _ = x_sample.shape
    n_ab, n_cd = w_in_ab.shape[0], w_in_cd.shape[0]
    d_a = conv_a_w.shape[-1]
    d_c = conv_c_w.shape[-1]
    q_rank, kv_rank = q_norm.shape[-1], kv_norm.shape[-1]

    p = {
        'norm_mix_pre': norm_mix_pre, 'norm_mix_post': norm_mix_post, 'norm_ffn_pre': norm_ffn_pre,
        'norm_ffn_post': norm_ffn_post, 'conv_a_w': conv_a_w, 'q_norm': q_norm, 'kv_norm': kv_norm,
        'w_out_ab': w_out_ab.astype(BF16), 'w_in_cd': w_in_cd.astype(BF16), 'conv_c_w': conv_c_w,
        'conv_c_b': conv_c_b, 'ln_c_g': ln_c_g, 'ln_c_b': ln_c_b, 'ln_v_g': ln_v_g, 'ln_v_b': ln_v_b,
        'w_spatial': w_spatial, 'b_spatial': b_spatial, 'w_out_cd': w_out_cd.astype(BF16),
        'w_ffn_gate': w_ffn_gate.astype(BF16), 'w_ffn_up': w_ffn_up.astype(BF16),
        'w_ffn_down': w_ffn_down.astype(BF16),
        'ab': [_prep_ab_weights(w_in_ab[i], w_q_up[i], w_kv_up[i], d_a, q_rank, kv_rank) for i in range(n_ab)],
    }

    mods = _modulation(jnp.concatenate([c_prompt, c_sample], axis=0), w_mod, b_mod)
    mods_p, mods_s = mods[:, :b_p], mods[:, b_p:]

    zeros_a = jnp.zeros((n_ab, b_p, state_conv_a.shape[2], d_a), F32)
    zeros_c = jnp.zeros((n_cd, b_p, state_conv_c.shape[2], d_c), F32)

    def attend_prompt(i, q_cat, ckv, kpe, k_cat, w_uv_pad):
        return _prompt_attention(q_cat, k_cat, w_uv_pad, b_p, s_p)

    y_p, sa_p, ckv_p, kpe_p, sc_p, _ = _trunk(
        x_prompt, mods_p, jnp.arange(s_p), zeros_a, zeros_c, attend_prompt, CHUNK, False, BF16, p)

    past_len = page_table.shape[1] * PAGE_SIZE

    cache_kpe_t = jnp.swapaxes(cache_kpe, 2, 3)

    def attend_sample(i, q_cat, ckv, kpe, k_cat, w_uv_pad):
        return _paged_attention(q_cat, ckv, kpe, cache_ckv, cache_kpe_t, page_table, i, w_uv_pad, s_s)

    y_s, sa_s, ckv_s, kpe_s, sc_s, vd_s = _trunk(
        x_sample, mods_s, past_len + jnp.arange(s_s), state_conv_a, state_conv_c, attend_sample, s_s,
        True, F32, p)

    st = jnp.stack
    return (y_p, y_s, st(sa_p), st(ckv_p), st(kpe_p), st(sc_p), st(sa_s), st(ckv_s), st(kpe_s), st(sc_s),
            st(vd_s))
```

```python
import functools

import jax
import jax.numpy as jnp
from jax import lax
from jax.experimental import pallas as pl
from jax.experimental.pallas import tpu as pltpu

N_HEADS_B = 8
NOPE_DIM = 64
ROPE_DIM = 32
V_DIM = 64
ROPE_THETA = 10000.0
G_D = 8
CHUNK = 128
PAGE_SIZE = 128
EPS = 1e-6
NEG_INF = -1e30
SCALE_B = (NOPE_DIM + ROPE_DIM) ** -0.5

LANES = 128
SUBLANES = 8
MXU_COLS = 256
VMEM_LIMIT = 56 * 1024 * 1024

ROW_TILE = 512
ATTN_Q_TILE = 256
PAGES_PER_CHUNK = 32
PAGED_SLOTS = 4
PAGED_SEQS_PER_STEP = 4

F32 = jnp.float32
BF16 = jnp.bfloat16


def _rms(x, g):
    return x * lax.rsqrt(jnp.mean(x * x, axis=-1, keepdims=True) + EPS) * g


def _ln(x, g, b):
    xc = x - jnp.mean(x, axis=-1, keepdims=True)
    return xc * lax.rsqrt(jnp.mean(xc * xc, axis=-1, keepdims=True) + EPS) * g + b


def _silu(x):
    return x * jax.nn.sigmoid(x)


def _dot(a, b):
    return jnp.dot(a.astype(BF16), b.astype(BF16), preferred_element_type=F32)


def _dot_nt(a, b):
    return lax.dot_general(a.astype(BF16), b.astype(BF16), (((1,), (1,)), ((), ())),
                           preferred_element_type=F32)


def _params(*sem):
    return pltpu.CompilerParams(dimension_semantics=sem, vmem_limit_bytes=VMEM_LIMIT)


def _mod_kernel(c_ref, w_ref, b_ref, o_ref):
    o_ref[0] = _dot(_silu(c_ref[...]), w_ref[0]) + b_ref[0]


def _modulation(c_all, w_mod, b_mod):
    n_l, d, n = w_mod.shape
    bc = c_all.shape[0]
    tn = n // 4
    return pl.pallas_call(
        _mod_kernel,
        grid=(n_l, n // tn),
        in_specs=[pl.BlockSpec((bc, d), lambda l, j: (0, 0)),
                  pl.BlockSpec((1, d, tn), lambda l, j: (l, 0, j)),
                  pl.BlockSpec((1, 1, tn), lambda l, j: (l, 0, j))],
        out_specs=pl.BlockSpec((1, bc, tn), lambda l, j: (l, 0, j)),
        out_shape=jax.ShapeDtypeStruct((n_l, bc, n), F32),
        compiler_params=_params("arbitrary", "arbitrary"),
    )(c_all, w_mod, b_mod.reshape(n_l, 1, n))


def _mod_spec(bt, d, k, grid_rank):
    if grid_rank == 2:
        return pl.BlockSpec((bt, 1, d), lambda b, s: (b, 0, k))
    return pl.BlockSpec((bt, 1, d), lambda b, s, f: (b, 0, k))


def _const_spec(shape, grid_rank):
    zeros = (0,) * len(shape)
    if grid_rank == 2:
        return pl.BlockSpec(shape, lambda b, s: zeros)
    return pl.BlockSpec(shape, lambda b, s, f: zeros)


def _tiles(b, s):
    if s >= ROW_TILE:
        assert s % ROW_TILE == 0
        return 1, ROW_TILE
    assert ROW_TILE % s == 0 and s % SUBLANES == 0 and b % (ROW_TILE // s) == 0
    return ROW_TILE // s, s


def _ab_in_kernel(x_ref, sh_ref, sc_ref, g_ref, w_ref, cw_ref, hist_ref, qn_ref, wq_ref, kvn_ref,
                  wuk_ref, cosq_ref, sinq_ref, ropek_ref,
                  ya_ref, qcat_ref, ckv_ref, kpe_ref, kcat_ref, nconv_ref,
                  zs_ref, *, bt, ts, d_a, q_rank, kv_rank):
    tm = bt * ts
    hist_rows = nconv_ref.shape[1]
    x = x_ref[...]
    h = _rms(x, g_ref[...]) * (1.0 + sc_ref[...]) + sh_ref[...]
    proj = _dot(h.reshape(tm, h.shape[-1]), w_ref[...])
    o_cq = 3 * d_a
    o_ckv = o_cq + q_rank
    o_kpe = o_ckv + kv_rank
    gate_out = proj[:, 0:d_a].reshape(bt, ts, d_a)
    z = (proj[:, d_a:2 * d_a] * proj[:, 2 * d_a:3 * d_a]).reshape(bt, ts, d_a)

    base = SUBLANES - hist_rows

    @pl.when(pl.program_id(1) == 0)
    def _():
        zs_ref[:, base:SUBLANES, :] = hist_ref[...]

    zs_ref[:, SUBLANES:SUBLANES + ts, :] = z
    cw = cw_ref[...]
    conv = z * cw[hist_rows:hist_rows + 1, :]
    for k in range(hist_rows):
        conv = conv + zs_ref[:, base + k:base + k + ts, :] * cw[k:k + 1, :]
    ya_ref[...] = (gate_out * conv).reshape(tm, d_a).astype(ya_ref.dtype)
    tail = zs_ref[:, ts + base:ts + SUBLANES, :]
    nconv_ref[...] = tail
    zs_ref[:, base:SUBLANES, :] = tail

    cqn = _rms(proj[:, o_cq:o_ckv], qn_ref[...])
    q = _dot(cqn, wq_ref[...]) * SCALE_B
    n_nope = N_HEADS_B * NOPE_DIM
    n_rope = N_HEADS_B * ROPE_DIM
    qr = q[:, n_nope:n_nope + n_rope].reshape(bt, ts, n_rope)
    qs = q[:, n_nope + n_rope:n_nope + 2 * n_rope].reshape(bt, ts, n_rope)
    qpe = (qr * cosq_ref[...][None] + qs * sinq_ref[...][None]).reshape(tm, n_rope)
    for hh in range(N_HEADS_B):
        pair = q[:, LANES * (hh // 2):LANES * (hh // 2 + 1)]
        q_lat = _dot(pair, wuk_ref[hh])
        qcat_ref[hh] = jnp.concatenate(
            [q_lat, qpe[:, hh * ROPE_DIM:(hh + 1) * ROPE_DIM]], axis=-1).astype(qcat_ref.dtype)

    ckvn = _rms(proj[:, o_ckv:o_kpe], kvn_ref[...])
    ckv_ref[...] = ckvn
    kt = proj[:, o_kpe:o_kpe + LANES].reshape(bt, ts, LANES) * ropek_ref[...][None]
    kpe = (kt[:, :, 0:ROPE_DIM] + kt[:, :, ROPE_DIM:2 * ROPE_DIM]).reshape(tm, ROPE_DIM)
    kpe_ref[...] = kpe
    kcat_ref[...] = jnp.concatenate([ckvn, kpe], axis=-1).astype(BF16)


def _ab_in(x, mod3, g_pre, w_in_ext, conv_w, hist, q_norm, w_q_perm, kv_norm, w_uk_pad,
           cosq, sinq, ropek, ql_dtype):
    b, s, d = x.shape
    m = b * s
    bt, ts = _tiles(b, s)
    tm = bt * ts
    d_a = conv_w.shape[1]
    q_rank = q_norm.shape[-1]
    kv_rank = kv_norm.shape[-1]
    hist_rows = hist.shape[1]
    n_ext = w_in_ext.shape[1]
    grid = (b // bt, s // ts)
    row = lambda bb, ss: (bb * (s // ts) + ss, 0)
    kern = functools.partial(_ab_in_kernel, bt=bt, ts=ts, d_a=d_a, q_rank=q_rank, kv_rank=kv_rank)
    d_qk = kv_rank + ROPE_DIM
    out_shape = (
        jax.ShapeDtypeStruct((m, d_a), BF16),
        jax.ShapeDtypeStruct((N_HEADS_B, m, d_qk), ql_dtype),
        jax.ShapeDtypeStruct((m, kv_rank), F32),
        jax.ShapeDtypeStruct((m, ROPE_DIM), F32),
        jax.ShapeDtypeStruct((m, d_qk), BF16),
        jax.ShapeDtypeStruct((b, hist_rows, d_a), F32),
    )
    out_specs = (
        pl.BlockSpec((tm, d_a), row),
        pl.BlockSpec((N_HEADS_B, tm, d_qk), lambda bb, ss: (0, bb * (s // ts) + ss, 0)),
        pl.BlockSpec((tm, kv_rank), row),
        pl.BlockSpec((tm, ROPE_DIM), row),
        pl.BlockSpec((tm, d_qk), row),
        pl.BlockSpec((bt, hist_rows, d_a), lambda bb, ss: (bb, 0, 0)),
    )
    in_specs = [
        pl.BlockSpec((bt, ts, d), lambda bb, ss: (bb, ss, 0)),
        _mod_spec(bt, d, 0, 2), _mod_spec(bt, d, 1, 2),
        _const_spec((1, d), 2),
        _const_spec((d, n_ext), 2),
        _const_spec(conv_w.shape, 2),
        pl.BlockSpec((bt, hist_rows, d_a), lambda bb, ss: (bb, 0, 0)),
        _const_spec((1, q_rank), 2),
        _const_spec(w_q_perm.shape, 2),
        _const_spec((1, kv_rank), 2),
        _const_spec(w_uk_pad.shape, 2),
        pl.BlockSpec((ts, cosq.shape[1]), lambda bb, ss: (ss, 0)),
        pl.BlockSpec((ts, sinq.shape[1]), lambda bb, ss: (ss, 0)),
        pl.BlockSpec((ts, LANES), lambda bb, ss: (ss, 0)),
    ]
    return pl.pallas_call(
        kern, grid=grid, in_specs=in_specs, out_specs=out_specs, out_shape=out_shape,
        scratch_shapes=[pltpu.VMEM((bt, ts + SUBLANES, d_a), F32)],
        compiler_params=_params("arbitrary", "arbitrary"),
    )(x, mod3, mod3, g_pre.reshape(1, d), w_in_ext, conv_w, hist, q_norm.reshape(1, q_rank),
      w_q_perm, kv_norm.reshape(1, kv_rank), w_uk_pad, cosq, sinq, ropek)


def _fold_lanes(x, op):
    out = x[:, 0:LANES]
    for t in range(1, x.shape[1] // LANES):
        out = op(out, x[:, t * LANES:(t + 1) * LANES])
    return out


def _value_up(o, wuv_ref, rows):
    parts = []
    for j in range(N_HEADS_B // 2):
        o0 = o[(2 * j) * rows:(2 * j + 1) * rows]
        o1 = o[(2 * j + 1) * rows:(2 * j + 2) * rows]
        parts.append(_dot(o0, wuv_ref[2 * j]) + _dot(o1, wuv_ref[2 * j + 1]))
    return jnp.concatenate(parts, axis=-1)


def _prompt_attn_kernel(q_ref, k_ref, wuv_ref, yb_ref, s_ref, stat_ref, lsum_ref, acc_ref, *, tq, r):
    i = pl.program_id(1)
    rows = N_HEADS_B * tq
    q = q_ref[...].reshape(rows, q_ref.shape[-1])

    def key_block(j):
        return k_ref[pl.ds(pl.multiple_of(j * tq, tq), tq), :]

    def scores_pass(j, mp):
        s = _dot_nt(q, key_block(j))
        s_ref[j] = s
        return jnp.maximum(mp, _fold_lanes(s, jnp.maximum))

    stat_ref[...] = jnp.full(stat_ref.shape, NEG_INF, F32)

    def scores_body(j, carry):
        stat_ref[...] = scores_pass(j, stat_ref[...])
        return carry

    lax.fori_loop(0, i, scores_body, 0)
    s = _dot_nt(q, key_block(i))
    qpos = lax.broadcasted_iota(jnp.int32, s.shape, 0) & (tq - 1)
    kpos = lax.broadcasted_iota(jnp.int32, s.shape, 1)
    s = jnp.where(kpos <= qpos, s, NEG_INF)
    s_ref[i] = s
    mp = jnp.maximum(stat_ref[...], _fold_lanes(s, jnp.maximum))
    m = jnp.max(mp, axis=-1, keepdims=True)
    stat_ref[...] = jnp.broadcast_to(m, stat_ref.shape)
    acc_ref[...] = jnp.zeros(acc_ref.shape, F32)

    lsum_ref[...] = jnp.zeros(lsum_ref.shape, F32)

    def value_body(j, carry):
        mrow = stat_ref[...]
        p = jnp.exp(s_ref[j] - jnp.concatenate([mrow] * (tq // LANES), axis=-1))
        acc_ref[...] += jnp.dot(p.astype(BF16), key_block(j)[:, 0:r], preferred_element_type=F32)
        lsum_ref[...] += _fold_lanes(p, jnp.add)
        return carry

    lax.fori_loop(0, i + 1, value_body, 0)
    o = acc_ref[...] / jnp.sum(lsum_ref[...], axis=-1, keepdims=True)
    yb_ref[...] = _value_up(o, wuv_ref, tq).astype(yb_ref.dtype)


def _prompt_attention(q_cat, k_cat, w_uv_pad, b, s):
    n_h, m, d_qk = q_cat.shape
    r = d_qk - ROPE_DIM
    tq = ATTN_Q_TILE
    nq = s // tq
    rows = n_h * tq
    kern = functools.partial(_prompt_attn_kernel, tq=tq, r=r)
    return pl.pallas_call(
        kern, grid=(b, nq),
        in_specs=[pl.BlockSpec((n_h, tq, d_qk), lambda bb, i: (0, bb * nq + i, 0)),
                  pl.BlockSpec((s, d_qk), lambda bb, i: (bb, 0)),
                  _const_spec(w_uv_pad.shape, 2)],
        out_specs=pl.BlockSpec((tq, n_h * V_DIM), lambda bb, i: (bb * nq + i, 0)),
        out_shape=jax.ShapeDtypeStruct((m, n_h * V_DIM), BF16),
        scratch_shapes=[pltpu.VMEM((nq, rows, tq), F32), pltpu.VMEM((rows, LANES), F32),
                        pltpu.VMEM((rows, LANES), F32), pltpu.VMEM((rows, r), F32)],
        compiler_params=_params("arbitrary", "arbitrary"),
    )(q_cat, k_cat, w_uv_pad)


def _paged_attn_kernel(pt_ref, q_ref, cnew_ref, knew_ref, wuv_ref, ckv_hbm, kpet_hbm, yb_ref,
                       raw_k, raw_p, kb_ref, s_ref, sems, *, layer, n_chunks, n_slots, pages_chunk, page,
                       s_new, r, seqs_step):
    t = pl.program_id(0)
    rows = N_HEADS_B * s_new
    total = seqs_step * n_chunks
    q_lats, q_pes = [], []
    for g in range(seqs_step):
        qc = q_ref[:, g * s_new:(g + 1) * s_new, :].reshape(rows, q_ref.shape[-1])
        q_lats.append(qc[:, 0:r].astype(BF16))
        q_pes.append(qc[:, r:].astype(BF16))

    def chunk_copies(n, slot):
        seq = t * seqs_step + n // n_chunks
        c = n % n_chunks
        return chunk_copies_of(seq, c, slot)

    def chunk_copies_of(seq, c, slot):
        copies = []
        for i in range(pages_chunk):
            pg = pt_ref[seq, c * pages_chunk + i]
            copies.append(pltpu.make_async_copy(
                ckv_hbm.at[layer, pg], raw_k.at[slot, pl.ds(i * page, page)], sems.at[0, slot]))
            copies.append(pltpu.make_async_copy(
                kpet_hbm.at[layer, pg], raw_p.at[slot, :, pl.ds(i * page, page)], sems.at[1, slot]))
        return copies

    def start_chunk(n, slot):
        for i, cp in enumerate(chunk_copies(n, slot)):
            cp.start(priority=(i // 2) % 2)

    ahead = n_slots - 1

    @pl.when(t == 0)
    def _():
        for n in range(ahead):
            start_chunk(n, n % n_slots)

    parts = [[] for _ in range(seqs_step)]

    def finish(n):
        s = s_ref[n % 2]
        m_c = jnp.max(_fold_lanes(s, jnp.maximum), axis=-1, keepdims=True)
        p = jnp.exp(s - m_c)
        l_c = jnp.sum(_fold_lanes(p, jnp.add), axis=-1, keepdims=True)
        o_c = jnp.dot(p.astype(BF16), kb_ref[n % 2], preferred_element_type=F32)
        g = n // n_chunks
        parts[g].append((m_c, l_c, o_c))
        if n % n_chunks == n_chunks - 1:
            merge(g)

    def merge(g):
        tok = slice(g * s_new, (g + 1) * s_new)
        k_new = cnew_ref[tok, :].astype(BF16)
        s_n = _dot_nt(q_lats[g], k_new) + _dot_nt(q_pes[g], knew_ref[tok, :])
        qpos = lax.broadcasted_iota(jnp.int32, s_n.shape, 0) & (s_new - 1)
        kpos = lax.broadcasted_iota(jnp.int32, s_n.shape, 1)
        s_n = jnp.where(kpos <= qpos, s_n, NEG_INF)
        m_n = jnp.max(s_n, axis=-1, keepdims=True)
        p_n = jnp.exp(s_n - m_n)
        parts[g].append((m_n, jnp.sum(p_n, axis=-1, keepdims=True),
                         jnp.dot(p_n.astype(BF16), k_new, preferred_element_type=F32)))
        m = parts[g][0][0]
        for m_c, _, _ in parts[g][1:]:
            m = jnp.maximum(m, m_c)
        l = jnp.zeros((rows, 1), F32)
        acc = jnp.zeros((rows, r), F32)
        for m_c, l_c, o_c in parts[g]:
            w = jnp.exp(m_c - m)
            l = l + w * l_c
            acc = acc + w * o_c
        yb_ref[tok, :] = _value_up(acc / l, wuv_ref, s_new).astype(yb_ref.dtype)

    for n in range(total):
        slot = n % n_slots
        pltpu.make_async_copy(raw_k.at[slot], raw_k.at[slot], sems.at[0, slot]).wait()
        pltpu.make_async_copy(raw_p.at[slot], raw_p.at[slot], sems.at[1, slot]).wait()
        nxt = n + ahead
        if nxt < total:
            start_chunk(nxt, nxt % n_slots)
        else:
            @pl.when(t + 1 < pl.num_programs(0))
            def _():
                start_chunk(nxt, nxt % n_slots)
        k = raw_k[slot].astype(BF16)
        kb_ref[n % 2] = k
        g = n // n_chunks
        s_ref[n % 2] = (_dot_nt(q_lats[g], k)
                        + jnp.dot(q_pes[g], raw_p[slot].astype(BF16), preferred_element_type=F32))
        if n:
            finish(n - 1)
    finish(total - 1)


def _paged_attention(q_cat, ckv_new, kpe_new, cache_ckv, cache_kpe_t, page_table, layer, w_uv_pad, s_new):
    n_h, m, d_qk = q_cat.shape
    r = d_qk - ROPE_DIM
    b, n_pages = page_table.shape
    page = cache_ckv.shape[2]
    pages_chunk = min(PAGES_PER_CHUNK, n_pages)
    n_chunks = n_pages // pages_chunk
    n_slots = PAGED_SLOTS
    g = PAGED_SEQS_PER_STEP
    assert s_new & (s_new - 1) == 0 and n_pages % pages_chunk == 0
    assert b % g == 0 and (g * n_chunks) % n_slots == 0
    keys_chunk = pages_chunk * page
    rows = n_h * s_new
    kern = functools.partial(_paged_attn_kernel, layer=layer, n_chunks=n_chunks, n_slots=n_slots,
                             pages_chunk=pages_chunk, page=page, s_new=s_new, r=r, seqs_step=g)
    grid_spec = pltpu.PrefetchScalarGridSpec(
        num_scalar_prefetch=1, grid=(b // g,),
        in_specs=[pl.BlockSpec((n_h, g * s_new, d_qk), lambda bb, pt: (0, bb, 0)),
                  pl.BlockSpec((g * s_new, r), lambda bb, pt: (bb, 0)),
                  pl.BlockSpec((g * s_new, ROPE_DIM), lambda bb, pt: (bb, 0)),
                  pl.BlockSpec(w_uv_pad.shape, lambda bb, pt: (0, 0, 0)),
                  pl.BlockSpec(memory_space=pl.ANY),
                  pl.BlockSpec(memory_space=pl.ANY)],
        out_specs=pl.BlockSpec((g * s_new, n_h * V_DIM), lambda bb, pt: (bb, 0)),
        scratch_shapes=[pltpu.VMEM((n_slots, keys_chunk, r), F32),
                        pltpu.VMEM((n_slots, ROPE_DIM, keys_chunk), F32),
                        pltpu.VMEM((2, keys_chunk, r), BF16),
                        pltpu.VMEM((2, rows, keys_chunk), F32),
                        pltpu.SemaphoreType.DMA((2, n_slots))])
    return pl.pallas_call(
        kern, grid_spec=grid_spec,
        out_shape=jax.ShapeDtypeStruct((m, n_h * V_DIM), F32),
        compiler_params=_params("arbitrary"),
    )(page_table, q_cat, ckv_new, kpe_new, w_uv_pad, cache_ckv, cache_kpe_t)


def _cd_in_kernel(x_ref, sh_ref, sc_ref, g_ref, w_ref, cw_ref, cb_ref, lcg_ref, lcb_ref, lvg_ref, lvb_ref,
                  hist_ref, wsp_ref, bsp_ref, *rest, bt, ts, d_c, with_v):
    if with_v:
        yc_ref, yd_ref, nconv_ref, vn_ref, hs_ref = rest
    else:
        yc_ref, yd_ref, nconv_ref, hs_ref = rest
    tm = bt * ts
    hist_rows = nconv_ref.shape[1]
    n_taps = hist_rows + 1
    pad = hs_ref.shape[1] - ts
    base = pad - hist_rows
    x = x_ref[...]
    h = _rms(x, g_ref[...]) * (1.0 + sc_ref[...]) + sh_ref[...]
    proj = _dot(h.reshape(tm, h.shape[-1]), w_ref[...])

    glu = (proj[:, 0:d_c] * jax.nn.sigmoid(proj[:, d_c:2 * d_c])).reshape(bt, ts, d_c)

    @pl.when(pl.program_id(1) == 0)
    def _():
        if base:
            hs_ref[:, 0:base, :] = jnp.zeros((bt, base, d_c), F32)
        hs_ref[:, base:pad, :] = hist_ref[...]

    hs_ref[:, pad:pad + ts, :] = glu
    cw = cw_ref[...]
    conv = glu * cw[n_taps - 1:n_taps, :] + cb_ref[...]
    for rem in range(SUBLANES):
        part = None
        for row0 in range(0, pad, SUBLANES):
            k = row0 + rem - base
            if 0 <= k < n_taps - 1:
                term = hs_ref[:, row0:row0 + ts + SUBLANES, :] * cw[k:k + 1, :]
                part = term if part is None else part + term
        if part is not None:
            conv = conv + part[:, rem:rem + ts, :]
    yc = _silu(_ln(conv, lcg_ref[...], lcb_ref[...]))
    yc_ref[...] = yc.reshape(tm, d_c).astype(yc_ref.dtype)
    tail = hs_ref[:, ts + base:ts + pad, :]
    nconv_ref[...] = tail
    hs_ref[:, base:pad, :] = tail

    u = proj[:, 2 * d_c:3 * d_c]
    vn = _ln(proj[:, 3 * d_c:4 * d_c], lvg_ref[...], lvb_ref[...])
    if with_v:
        vn_ref[...] = vn
    low = lax.broadcasted_iota(jnp.int32, (CHUNK, LANES), 1) < (LANES // 2)
    bsp = bsp_ref[...]
    for c in range(tm // CHUNK):
        parts = []
        for j in range(d_c // LANES):
            vj = vn[c * CHUNK:(c + 1) * CHUNK, j * LANES:(j + 1) * LANES]
            rhs = jnp.concatenate([jnp.where(low, vj, 0.0), jnp.where(low, 0.0, vj)], axis=0)
            parts.append(_dot(wsp_ref[j], rhs))
        sv = jnp.concatenate(parts, axis=-1) + bsp
        yd_ref[c * CHUNK:(c + 1) * CHUNK, :] = (u[c * CHUNK:(c + 1) * CHUNK] * sv).astype(yd_ref.dtype)


def _cd_in(x, mod3, g_pre, w_in, conv_w, conv_b, ln_c_g, ln_c_b, ln_v_g, ln_v_b, hist, w_sp_cat, b_sp_rows,
           with_v):
    b, s, d = x.shape
    m = b * s
    bt, ts = _tiles(b, s)
    tm = bt * ts
    d_c = conv_w.shape[1]
    hist_rows = hist.shape[1]
    pad = -(-hist_rows // SUBLANES) * SUBLANES
    grid = (b // bt, s // ts)
    row = lambda bb, ss: (bb * (s // ts) + ss, 0)
    vec = lambda a: a.reshape(1, -1)
    kern = functools.partial(_cd_in_kernel, bt=bt, ts=ts, d_c=d_c, with_v=with_v)
    out_shape = [jax.ShapeDtypeStruct((m, d_c), BF16), jax.ShapeDtypeStruct((m, d_c), BF16),
                 jax.ShapeDtypeStruct((b, hist_rows, d_c), F32)]
    out_specs = [pl.BlockSpec((tm, d_c), row), pl.BlockSpec((tm, d_c), row),
                 pl.BlockSpec((bt, hist_rows, d_c), lambda bb, ss: (bb, 0, 0))]
    if with_v:
        out_shape.append(jax.ShapeDtypeStruct((m, d_c), F32))
        out_specs.append(pl.BlockSpec((tm, d_c), row))
    in_specs = [
        pl.BlockSpec((bt, ts, d), lambda bb, ss: (bb, ss, 0)),
        _mod_spec(bt, d, 0, 2), _mod_spec(bt, d, 1, 2),
        _const_spec((1, d), 2),
        _const_spec(w_in.shape, 2),
        _const_spec(conv_w.shape, 2),
        _const_spec((1, d_c), 2), _const_spec((1, d_c), 2), _const_spec((1, d_c), 2),
        _const_spec((1, d_c), 2), _const_spec((1, d_c), 2),
        pl.BlockSpec((bt, hist_rows, d_c), lambda bb, ss: (bb, 0, 0)),
        _const_spec(w_sp_cat.shape, 2),
        _const_spec(b_sp_rows.shape, 2),
    ]
    return pl.pallas_call(
        kern, grid=grid, in_specs=in_specs, out_specs=tuple(out_specs), out_shape=tuple(out_shape),
        scratch_shapes=[pltpu.VMEM((bt, ts + pad, d_c), F32)],
        compiler_params=_params("arbitrary", "arbitrary"),
    )(x, mod3, mod3, vec(g_pre), w_in, conv_w, vec(conv_b), vec(ln_c_g), vec(ln_c_b), vec(ln_v_g),
      vec(ln_v_b), hist, w_sp_cat, b_sp_rows)


def _out_ffn_kernel(y0_ref, y1_ref, x_ref, gm_ref, shf_ref, scf_ref, gf_ref, npost_ref, nfpre_ref, nfpost_ref,
                    wo_ref, wg_ref, wu_ref, wd_ref, o_ref, *, bt, ts, ff_splits):
    tm = bt * ts
    d = x_ref.shape[-1]
    k0 = y0_ref.shape[-1]
    y = _dot(y0_ref[...], wo_ref[0:k0, :]) + _dot(y1_ref[...], wo_ref[k0:, :])
    x1 = x_ref[...] + gm_ref[...] * _rms(y, npost_ref[...]).reshape(bt, ts, d)
    h = (_rms(x1, nfpre_ref[...]) * (1.0 + scf_ref[...]) + shf_ref[...]).reshape(tm, d).astype(BF16)
    ff = None
    for lo, hi in ff_splits:
        a = _silu(_dot(h, wg_ref[:, lo:hi])) * _dot(h, wu_ref[:, lo:hi])
        part = _dot(a, wd_ref[lo:hi, :])
        ff = part if ff is None else ff + part
    o_ref[...] = x1 + gf_ref[...] * _rms(ff, nfpost_ref[...]).reshape(bt, ts, d)


def _out_ffn(y0, y1, x, mod3, norm_post, norm_ffn_pre, norm_ffn_post, w_out, w_gate, w_up, w_down, layer):
    b, s, d = x.shape
    bt, ts = _tiles(b, s)
    tm = bt * ts
    d_ff = w_gate.shape[2]
    assert d_ff % MXU_COLS == 0
    n_tiles = d_ff // MXU_COLS
    cut = (n_tiles + 1) // 2 * MXU_COLS
    ff_splits = ((0, cut), (cut, d_ff)) if cut < d_ff else ((0, d_ff),)
    grid = (b // bt, s // ts)
    row = lambda bb, ss: (bb * (s // ts) + ss, 0)
    vec = lambda a: a.reshape(1, -1)
    once = pl.Buffered(1)
    kern = functools.partial(_out_ffn_kernel, bt=bt, ts=ts, ff_splits=ff_splits)
    in_specs = [
        pl.BlockSpec((tm, y0.shape[1]), row),
        pl.BlockSpec((tm, y1.shape[1]), row),
        pl.BlockSpec((bt, ts, d), lambda bb, ss: (bb, ss, 0)),
        _mod_spec(bt, d, 2, 2), _mod_spec(bt, d, 3, 2), _mod_spec(bt, d, 4, 2), _mod_spec(bt, d, 5, 2),
        _const_spec((1, d), 2), _const_spec((1, d), 2), _const_spec((1, d), 2),
        pl.BlockSpec(w_out.shape, lambda bb, ss: (0, 0), pipeline_mode=once),
        pl.BlockSpec((None, d, d_ff), lambda bb, ss: (layer, 0, 0), pipeline_mode=once),
        pl.BlockSpec((None, d, d_ff), lambda bb, ss: (layer, 0, 0), pipeline_mode=once),
        pl.BlockSpec((None, d_ff, d), lambda bb, ss: (layer, 0, 0), pipeline_mode=once),
    ]
    return pl.pallas_call(
        kern, grid=grid, in_specs=in_specs,
        out_specs=pl.BlockSpec((bt, ts, d), lambda bb, ss: (bb, ss, 0)),
        out_shape=jax.ShapeDtypeStruct((b, s, d), F32),
        compiler_params=_params("arbitrary", "arbitrary"),
    )(y0, y1, x, mod3, mod3, mod3, mod3, vec(norm_post), vec(norm_ffn_pre), vec(norm_ffn_post),
      w_out, w_gate, w_up, w_down)


def _rope_tables(pos):
    inv_freq = ROPE_THETA ** (-jnp.arange(0, ROPE_DIM, 2, dtype=F32) / ROPE_DIM)
    ang = pos.astype(F32)[:, None] * inv_freq[None, :]
    cos, sin = jnp.cos(ang), jnp.sin(ang)
    cosq = jnp.tile(jnp.concatenate([cos, cos], axis=-1), (1, N_HEADS_B))
    sinq = jnp.tile(jnp.concatenate([-sin, sin], axis=-1), (1, N_HEADS_B))
    zeros = jnp.zeros((pos.shape[0], LANES - 2 * ROPE_DIM), F32)
    ropek = jnp.concatenate([cos, cos, -sin, sin, zeros], axis=-1)
    return cosq, sinq, ropek


def _prep_ab_weights(w_in, w_q_up, w_kv_up, d_a, q_rank, kv_rank):
    d = w_in.shape[0]
    o_kpe = 3 * d_a + q_rank + kv_rank
    half = ROPE_DIM // 2
    swapped = jnp.concatenate([w_in[:, o_kpe + half:o_kpe + ROPE_DIM], w_in[:, o_kpe:o_kpe + half]], axis=1)
    w_in_ext = jnp.concatenate(
        [w_in, swapped, jnp.zeros((d, LANES - 2 * ROPE_DIM), w_in.dtype)], axis=1).astype(BF16)
    wq = w_q_up.reshape(q_rank, N_HEADS_B, NOPE_DIM + ROPE_DIM)
    wq_rope = wq[:, :, NOPE_DIM:]
    wq_rope_sw = jnp.concatenate([wq_rope[:, :, half:], wq_rope[:, :, :half]], axis=2)
    w_q_perm = jnp.concatenate([
        wq[:, :, :NOPE_DIM].reshape(q_rank, -1),
        wq_rope.reshape(q_rank, -1),
        wq_rope_sw.reshape(q_rank, -1)], axis=1).astype(BF16)
    w_kv = w_kv_up.reshape(kv_rank, N_HEADS_B, NOPE_DIM + V_DIM)
    w_uk_t = jnp.transpose(w_kv[..., :NOPE_DIM], (1, 2, 0))
    w_uv = jnp.transpose(w_kv[..., NOPE_DIM:], (1, 0, 2))
    odd = (jnp.arange(N_HEADS_B) % 2 == 1)
    zk = jnp.zeros_like(w_uk_t)
    w_uk_pad = jnp.where(odd[:, None, None],
                         jnp.concatenate([zk, w_uk_t], axis=1),
                         jnp.concatenate([w_uk_t, zk], axis=1)).astype(BF16)
    zv = jnp.zeros_like(w_uv)
    w_uv_pad = jnp.where(odd[:, None, None],
                         jnp.concatenate([zv, w_uv], axis=2),
                         jnp.concatenate([w_uv, zv], axis=2)).astype(BF16)
    return w_in_ext, w_q_perm, w_uk_pad, w_uv_pad


def _prep_spatial(w_sp, b_sp, n):
    wm = jnp.tril(w_sp[:, :n, :n])
    reps = CHUNK // n
    eye = jnp.eye(reps, dtype=w_sp.dtype)
    big = jnp.einsum('ab,gts->gatbs', eye, wm).reshape(G_D, CHUNK, CHUNK)
    w_cat = jnp.concatenate([big[0::2], big[1::2]], axis=2).astype(BF16)
    b_rows = jnp.tile(b_sp[:, :n].T, (reps, 1))
    return w_cat, b_rows


def _trunk(x, mods, pos, hist_a, hist_c, attend, chunk_rows, with_v, ql_dtype, p):
    b, s, d = x.shape
    depth = p['w_ffn_gate'].shape[0]
    cosq, sinq, ropek = _rope_tables(pos)
    conv_a, ckvs, kpes, conv_c, vds = [], [], [], [], []
    for l in range(depth):
        mod3 = mods[l].reshape(b, 1, -1)
        i = l // 2
        if l % 2 == 0:
            w_in_ext, w_q_perm, w_uk_pad, w_uv_pad = p['ab'][i]
            ya, q_cat, ckv, kpe, k_cat, nconv = _ab_in(
                x, mod3, p['norm_mix_pre'][l], w_in_ext, p['conv_a_w'][i], hist_a[i], p['q_norm'][i],
                w_q_perm, p['kv_norm'][i], w_uk_pad, cosq, sinq, ropek, ql_dtype)
            y1 = attend(i, q_cat, ckv, kpe, k_cat, w_uv_pad)
            y0 = ya
            w_out = p['w_out_ab'][i]
            conv_a.append(nconv)
            ckvs.append(ckv.reshape(b, s, -1))
            kpes.append(kpe.reshape(b, s, -1))
        else:
            w_sp_cat, b_rows = _prep_spatial(p['w_spatial'][i], p['b_spatial'][i], chunk_rows)
            d_c = p['conv_c_w'].shape[-1]
            b_sp_rows = jnp.repeat(b_rows, d_c // G_D, axis=1)
            outs = _cd_in(x, mod3, p['norm_mix_pre'][l], p['w_in_cd'][i], p['conv_c_w'][i], p['conv_c_b'][i],
                          p['ln_c_g'][i], p['ln_c_b'][i], p['ln_v_g'][i], p['ln_v_b'][i], hist_c[i],
                          w_sp_cat, b_sp_rows, with_v)
            y0, y1, nconv = outs[:3]
            if with_v:
                vds.append(outs[3].reshape(b, s, -1))
            w_out = p['w_out_cd'][i]
            conv_c.append(nconv)
        x = _out_ffn(y0, y1, x, mod3, p['norm_mix_post'][l], p['norm_ffn_pre'][l], p['norm_ffn_post'][l],
                     w_out, p['w_ffn_gate'], p['w_ffn_up'], p['w_ffn_down'], l)
    return x, conv_a, ckvs, kpes, conv_c, vds


def kernel(x_prompt, x_sample, cache_ckv, cache_kpe, state_conv_a, state_conv_c, page_table, c_prompt, c_sample, w_mod, b_mod, norm_mix_pre, norm_mix_post, norm_ffn_pre, norm_ffn_post, w_in_ab, conv_a_w, q_norm, w_q_up, kv_norm, w_kv_up, w_out_ab, w_in_cd, conv_c_w, conv_c_b, ln_c_g, ln_c_b, ln_v_g, ln_v_b, w_spatial, b_spatial, w_out_cd, w_ffn_gate, w_ffn_up, w_ffn_down):
    b_p, s_p, d = x_prompt.shape
    b_s, s_s, _ = x_sample.shape
    n_ab, n_cd = w_in_ab.shape[0], w_in_cd.shape[0]
    d_a = conv_a_w.shape[-1]
    d_c = conv_c_w.shape[-1]
    q_rank, kv_rank = q_norm.shape[-1], kv_norm.shape[-1]

    p = {
        'norm_mix_pre': norm_mix_pre, 'norm_mix_post': norm_mix_post, 'norm_ffn_pre': norm_ffn_pre,
        'norm_ffn_post': norm_ffn_post, 'conv_a_w': conv_a_w, 'q_norm': q_norm, 'kv_norm': kv_norm,
        'w_out_ab': w_out_ab.astype(BF16), 'w_in_cd': w_in_cd.astype(BF16), 'conv_c_w': conv_c_w,
        'conv_c_b': conv_c_b, 'ln_c_g': ln_c_g, 'ln_c_b': ln_c_b, 'ln_v_g': ln_v_g, 'ln_v_b': ln_v_b,
        'w_spatial': w_spatial, 'b_spatial': b_spatial, 'w_out_cd': w_out_cd.astype(BF16),
        'w_ffn_gate': w_ffn_gate.astype(BF16), 'w_ffn_up': w_ffn_up.astype(BF16),
        'w_ffn_down': w_ffn_down.astype(BF16),
        'ab': [_prep_ab_weights(w_in_ab[i], w_q_up[i], w_kv_up[i], d_a, q_rank, kv_rank) for i in range(n_ab)],
    }

    mods = _modulation(jnp.concatenate([c_prompt, c_sample], axis=0), w_mod, b_mod)
    mods_p, mods_s = mods[:, :b_p], mods[:, b_p:]

    zeros_a = jnp.zeros((n_ab, b_p, state_conv_a.shape[2], d_a), F32)
    zeros_c = jnp.zeros((n_cd, b_p, state_conv_c.shape[2], d_c), F32)

    def attend_prompt(i, q_cat, ckv, kpe, k_cat, w_uv_pad):
        return _prompt_attention(q_cat, k_cat, w_uv_pad, b_p, s_p)

    y_p, sa_p, ckv_p, kpe_p, sc_p, _ = _trunk(
        x_prompt, mods_p, jnp.arange(s_p), zeros_a, zeros_c, attend_prompt, CHUNK, False, BF16, p)

    past_len = page_table.shape[1] * PAGE_SIZE

    cache_kpe_t = jnp.swapaxes(cache_kpe, 2, 3)

    def attend_sample(i, q_cat, ckv, kpe, k_cat, w_uv_pad):
        return _paged_attention(q_cat, ckv, kpe, cache_ckv, cache_kpe_t, page_table, i, w_uv_pad, s_s)

    y_s, sa_s, ckv_s, kpe_s, sc_s, vd_s = _trunk(
        x_sample, mods_s, past_len + jnp.arange(s_s), state_conv_a, state_conv_c, attend_sample, s_s,
        True, F32, p)

    st = jnp.stack
    return (y_p, y_s, st(sa_p), st(ckv_p), st(kpe_p), st(sc_p), st(sa_s), st(ckv_s), st(kpe_s), st(sc_s),
            st(vd_s))
```

```python
import functools

import jax
import jax.numpy as jnp
from jax import lax
from jax.experimental import pallas as pl
from jax.experimental.pallas import tpu as pltpu

N_HEADS_B = 8
NOPE_DIM = 64
ROPE_DIM = 32
V_DIM = 64
ROPE_THETA = 10000.0
G_D = 8
CHUNK = 128
PAGE_SIZE = 128
EPS = 1e-6
NEG_INF = -1e30
SCALE_B = (NOPE_DIM + ROPE_DIM) ** -0.5

LANES = 128
SUBLANES = 8
MXU_COLS = 256
VMEM_LIMIT = 56 * 1024 * 1024

ROW_TILE = 512
IN_ROW_TILE = 1024
ATTN_Q_TILE = 256
PAGES_PER_CHUNK = 32
PAGED_SLOTS = 4
PAGED_SEQS_PER_STEP = 4

F32 = jnp.float32
BF16 = jnp.bfloat16


def _rms(x, g):
    return x * lax.rsqrt(jnp.mean(x * x, axis=-1, keepdims=True) + EPS) * g


def _ln(x, g, b):
    xc = x - jnp.mean(x, axis=-1, keepdims=True)
    return xc * lax.rsqrt(jnp.mean(xc * xc, axis=-1, keepdims=True) + EPS) * g + b


def _silu(x):
    return x * jax.nn.sigmoid(x)


def _dot(a, b):
    return jnp.dot(a.astype(BF16), b.astype(BF16), preferred_element_type=F32)


def _dot_nt(a, b):
    return lax.dot_general(a.astype(BF16), b.astype(BF16), (((1,), (1,)), ((), ())),
                           preferred_element_type=F32)


def _params(*sem):
    return pltpu.CompilerParams(dimension_semantics=sem, vmem_limit_bytes=VMEM_LIMIT)


def _mod_kernel(c_ref, w_ref, b_ref, o_ref):
    o_ref[0] = _dot(_silu(c_ref[...]), w_ref[0]) + b_ref[0]


def _modulation(c_all, w_mod, b_mod):
    n_l, d, n = w_mod.shape
    bc = c_all.shape[0]
    tn = n // 4
    return pl.pallas_call(
        _mod_kernel,
        grid=(n_l, n // tn),
        in_specs=[pl.BlockSpec((bc, d), lambda l, j: (0, 0)),
                  pl.BlockSpec((1, d, tn), lambda l, j: (l, 0, j)),
                  pl.BlockSpec((1, 1, tn), lambda l, j: (l, 0, j))],
        out_specs=pl.BlockSpec((1, bc, tn), lambda l, j: (l, 0, j)),
        out_shape=jax.ShapeDtypeStruct((n_l, bc, n), F32),
        compiler_params=_params("arbitrary", "arbitrary"),
    )(c_all, w_mod, b_mod.reshape(n_l, 1, n))


def _mod_spec(bt, d, k, grid_rank):
    if grid_rank == 2:
        return pl.BlockSpec((bt, 1, d), lambda b, s: (b, 0, k))
    return pl.BlockSpec((bt, 1, d), lambda b, s, f: (b, 0, k))


def _const_spec(shape, grid_rank):
    zeros = (0,) * len(shape)
    if grid_rank == 2:
        return pl.BlockSpec(shape, lambda b, s: zeros)
    return pl.BlockSpec(shape, lambda b, s, f: zeros)


def _tiles(b, s, row_tile=ROW_TILE):
    if s >= row_tile:
        assert s % row_tile == 0
        return 1, row_tile
    assert row_tile % s == 0 and s % SUBLANES == 0 and b % (row_tile // s) == 0
    return row_tile // s, s


def _ab_in_kernel(x_ref, sh_ref, sc_ref, g_ref, w_ref, cw_ref, hist_ref, qn_ref, wq_ref, kvn_ref,
                  wuk_ref, cosq_ref, sinq_ref, ropek_ref,
                  ya_ref, qcat_ref, ckv_ref, kpe_ref, kcat_ref, nconv_ref,
                  zs_ref, *, bt, ts, d_a, q_rank, kv_rank):
    tm = bt * ts
    hist_rows = nconv_ref.shape[1]
    x = x_ref[...]
    h = _rms(x, g_ref[...]) * (1.0 + sc_ref[...]) + sh_ref[...]
    proj = _dot(h.reshape(tm, h.shape[-1]), w_ref[...])
    o_cq = 3 * d_a
    o_ckv = o_cq + q_rank
    o_kpe = o_ckv + kv_rank
    gate_out = proj[:, 0:d_a].reshape(bt, ts, d_a)
    z = (proj[:, d_a:2 * d_a] * proj[:, 2 * d_a:3 * d_a]).reshape(bt, ts, d_a)

    base = SUBLANES - hist_rows

    @pl.when(pl.program_id(1) == 0)
    def _():
        zs_ref[:, base:SUBLANES, :] = hist_ref[...]

    zs_ref[:, SUBLANES:SUBLANES + ts, :] = z
    cw = cw_ref[...]
    conv = z * cw[hist_rows:hist_rows + 1, :]
    for k in range(hist_rows):
        conv = conv + zs_ref[:, base + k:base + k + ts, :] * cw[k:k + 1, :]
    ya_ref[...] = (gate_out * conv).reshape(tm, d_a).astype(ya_ref.dtype)
    tail = zs_ref[:, ts + base:ts + SUBLANES, :]
    nconv_ref[...] = tail
    zs_ref[:, base:SUBLANES, :] = tail

    cqn = _rms(proj[:, o_cq:o_ckv], qn_ref[...])
    q = _dot(cqn, wq_ref[...]) * SCALE_B
    n_nope = N_HEADS_B * NOPE_DIM
    n_rope = N_HEADS_B * ROPE_DIM
    qr = q[:, n_nope:n_nope + n_rope].reshape(bt, ts, n_rope)
    qs = q[:, n_nope + n_rope:n_nope + 2 * n_rope].reshape(bt, ts, n_rope)
    qpe = (qr * cosq_ref[...][None] + qs * sinq_ref[...][None]).reshape(tm, n_rope)
    for hh in range(N_HEADS_B):
        pair = q[:, LANES * (hh // 2):LANES * (hh // 2 + 1)]
        q_lat = _dot(pair, wuk_ref[hh])
        qcat_ref[hh] = jnp.concatenate(
            [q_lat, qpe[:, hh * ROPE_DIM:(hh + 1) * ROPE_DIM]], axis=-1).astype(qcat_ref.dtype)

    ckvn = _rms(proj[:, o_ckv:o_kpe], kvn_ref[...])
    ckv_ref[...] = ckvn
    kt = proj[:, o_kpe:o_kpe + LANES].reshape(bt, ts, LANES) * ropek_ref[...][None]
    kpe = (kt[:, :, 0:ROPE_DIM] + kt[:, :, ROPE_DIM:2 * ROPE_DIM]).reshape(tm, ROPE_DIM)
    kpe_ref[...] = kpe
    kcat_ref[...] = jnp.concatenate([ckvn, kpe], axis=-1).astype(BF16)


def _ab_in(x, mod3, g_pre, w_in_ext, conv_w, hist, q_norm, w_q_perm, kv_norm, w_uk_pad,
           cosq, sinq, ropek, ql_dtype):
    b, s, d = x.shape
    m = b * s
    bt, ts = _tiles(b, s, IN_ROW_TILE if s >= IN_ROW_TILE else ROW_TILE)
    tm = bt * ts
    d_a = conv_w.shape[1]
    q_rank = q_norm.shape[-1]
    kv_rank = kv_norm.shape[-1]
    hist_rows = hist.shape[1]
    n_ext = w_in_ext.shape[1]
    grid = (b // bt, s // ts)
    row = lambda bb, ss: (bb * (s // ts) + ss, 0)
    kern = functools.partial(_ab_in_kernel, bt=bt, ts=ts, d_a=d_a, q_rank=q_rank, kv_rank=kv_rank)
    d_qk = kv_rank + ROPE_DIM
    out_shape = (
        jax.ShapeDtypeStruct((m, d_a), BF16),
        jax.ShapeDtypeStruct((N_HEADS_B, m, d_qk), ql_dtype),
        jax.ShapeDtypeStruct((m, kv_rank), F32),
        jax.ShapeDtypeStruct((m, ROPE_DIM), F32),
        jax.ShapeDtypeStruct((m, d_qk), BF16),
        jax.ShapeDtypeStruct((b, hist_rows, d_a), F32),
    )
    out_specs = (
        pl.BlockSpec((tm, d_a), row),
        pl.BlockSpec((N_HEADS_B, tm, d_qk), lambda bb, ss: (0, bb * (s // ts) + ss, 0)),
        pl.BlockSpec((tm, kv_rank), row),
        pl.BlockSpec((tm, ROPE_DIM), row),
        pl.BlockSpec((tm, d_qk), row),
        pl.BlockSpec((bt, hist_rows, d_a), lambda bb, ss: (bb, 0, 0)),
    )
    in_specs = [
        pl.BlockSpec((bt, ts, d), lambda bb, ss: (bb, ss, 0)),
        _mod_spec(bt, d, 0, 2), _mod_spec(bt, d, 1, 2),
        _const_spec((1, d), 2),
        _const_spec((d, n_ext), 2),
        _const_spec(conv_w.shape, 2),
        pl.BlockSpec((bt, hist_rows, d_a), lambda bb, ss: (bb, 0, 0)),
        _const_spec((1, q_rank), 2),
        _const_spec(w_q_perm.shape, 2),
        _const_spec((1, kv_rank), 2),
        _const_spec(w_uk_pad.shape, 2),
        pl.BlockSpec((ts, cosq.shape[1]), lambda bb, ss: (ss, 0)),
        pl.BlockSpec((ts, sinq.shape[1]), lambda bb, ss: (ss, 0)),
        pl.BlockSpec((ts, LANES), lambda bb, ss: (ss, 0)),
    ]
    return pl.pallas_call(
        kern, grid=grid, in_specs=in_specs, out_specs=out_specs, out_shape=out_shape,
        scratch_shapes=[pltpu.VMEM((bt, ts + SUBLANES, d_a), F32)],
        compiler_params=_params("arbitrary", "arbitrary"),
    )(x, mod3, mod3, g_pre.reshape(1, d), w_in_ext, conv_w, hist, q_norm.reshape(1, q_rank),
      w_q_perm, kv_norm.reshape(1, kv_rank), w_uk_pad, cosq, sinq, ropek)


def _fold_lanes(x, op):
    out = x[:, 0:LANES]
    for t in range(1, x.shape[1] // LANES):
        out = op(out, x[:, t * LANES:(t + 1) * LANES])
    return out


def _value_up(o, wuv_ref, rows):
    parts = []
    for j in range(N_HEADS_B // 2):
        o0 = o[(2 * j) * rows:(2 * j + 1) * rows]
        o1 = o[(2 * j + 1) * rows:(2 * j + 2) * rows]
        parts.append(_dot(o0, wuv_ref[2 * j]) + _dot(o1, wuv_ref[2 * j + 1]))
    return jnp.concatenate(parts, axis=-1)


def _prompt_attn_kernel(q_ref, k_ref, wuv_ref, yb_ref, s_ref, stat_ref, lsum_ref, acc_ref, *, tq, r):
    i = pl.program_id(1)
    rows = N_HEADS_B * tq
    q = q_ref[...].reshape(rows, q_ref.shape[-1])

    def key_block(j):
        return k_ref[pl.ds(pl.multiple_of(j * tq, tq), tq), :]

    def scores_pass(j, mp):
        s = _dot_nt(q, key_block(j))
        s_ref[j] = s
        return jnp.maximum(mp, _fold_lanes(s, jnp.maximum))

    stat_ref[...] = jnp.full(stat_ref.shape, NEG_INF, F32)

    def scores_body(j, carry):
        stat_ref[...] = scores_pass(j, stat_ref[...])
        return carry

    lax.fori_loop(0, i, scores_body, 0)
    s = _dot_nt(q, key_block(i))
    qpos = lax.broadcasted_iota(jnp.int32, s.shape, 0) & (tq - 1)
    kpos = lax.broadcasted_iota(jnp.int32, s.shape, 1)
    s = jnp.where(kpos <= qpos, s, NEG_INF)
    s_ref[i] = s
    mp = jnp.maximum(stat_ref[...], _fold_lanes(s, jnp.maximum))
    m = jnp.max(mp, axis=-1, keepdims=True)
    stat_ref[...] = jnp.broadcast_to(m, stat_ref.shape)
    acc_ref[...] = jnp.zeros(acc_ref.shape, F32)

    lsum_ref[...] = jnp.zeros(lsum_ref.shape, F32)

    def value_body(j, carry):
        mrow = stat_ref[...]
        p = jnp.exp(s_ref[j] - jnp.concatenate([mrow] * (tq // LANES), axis=-1))
        acc_ref[...] += jnp.dot(p.astype(BF16), key_block(j)[:, 0:r], preferred_element_type=F32)
        lsum_ref[...] += _fold_lanes(p, jnp.add)
        return carry

    lax.fori_loop(0, i + 1, value_body, 0)
    o = acc_ref[...] / jnp.sum(lsum_ref[...], axis=-1, keepdims=True)
    yb_ref[...] = _value_up(o, wuv_ref, tq).astype(yb_ref.dtype)


def _prompt_attention(q_cat, k_cat, w_uv_pad, b, s):
    n_h, m, d_qk = q_cat.shape
    r = d_qk - ROPE_DIM
    tq = ATTN_Q_TILE
    nq = s // tq
    rows = n_h * tq
    kern = functools.partial(_prompt_attn_kernel, tq=tq, r=r)
    return pl.pallas_call(
        kern, grid=(b, nq),
        in_specs=[pl.BlockSpec((n_h, tq, d_qk), lambda bb, i: (0, bb * nq + i, 0)),
                  pl.BlockSpec((s, d_qk), lambda bb, i: (bb, 0)),
                  _const_spec(w_uv_pad.shape, 2)],
        out_specs=pl.BlockSpec((tq, n_h * V_DIM), lambda bb, i: (bb * nq + i, 0)),
        out_shape=jax.ShapeDtypeStruct((m, n_h * V_DIM), BF16),
        scratch_shapes=[pltpu.VMEM((nq, rows, tq), F32), pltpu.VMEM((rows, LANES), F32),
                        pltpu.VMEM((rows, LANES), F32), pltpu.VMEM((rows, r), F32)],
        compiler_params=_params("arbitrary", "arbitrary"),
    )(q_cat, k_cat, w_uv_pad)


def _paged_attn_kernel(pt_ref, q_ref, cnew_ref, knew_ref, wuv_ref, ckv_hbm, kpet_hbm, yb_ref,
                       raw_k, raw_p, kb_ref, s_ref, sems, *, layer, n_chunks, n_slots, pages_chunk, page,
                       s_new, r, seqs_step):
    t = pl.program_id(0)
    rows = N_HEADS_B * s_new
    total = seqs_step * n_chunks
    q_lats, q_pes = [], []
    for g in range(seqs_step):
        qc = q_ref[:, g * s_new:(g + 1) * s_new, :].reshape(rows, q_ref.shape[-1])
        q_lats.append(qc[:, 0:r].astype(BF16))
        q_pes.append(qc[:, r:].astype(BF16))

    def chunk_copies(n, slot):
        seq = t * seqs_step + n // n_chunks
        c = n % n_chunks
        return chunk_copies_of(seq, c, slot)

    def chunk_copies_of(seq, c, slot):
        copies = []
        for i in range(pages_chunk):
            pg = pt_ref[seq, c * pages_chunk + i]
            copies.append(pltpu.make_async_copy(
                ckv_hbm.at[layer, pg], raw_k.at[slot, pl.ds(i * page, page)], sems.at[0, slot]))
            copies.append(pltpu.make_async_copy(
                kpet_hbm.at[layer, pg], raw_p.at[slot, :, pl.ds(i * page, page)], sems.at[1, slot]))
        return copies

    def start_chunk(n, slot):
        for i, cp in enumerate(chunk_copies(n, slot)):
            cp.start(priority=(i // 2) % 2)

    ahead = n_slots - 1

    @pl.when(t == 0)
    def _():
        for n in range(ahead):
            start_chunk(n, n % n_slots)

    parts = [[] for _ in range(seqs_step)]

    def finish(n):
        s = s_ref[n % 2]
        m_c = jnp.max(_fold_lanes(s, jnp.maximum), axis=-1, keepdims=True)
        p = jnp.exp(s - m_c)
        l_c = jnp.sum(_fold_lanes(p, jnp.add), axis=-1, keepdims=True)
        o_c = jnp.dot(p.astype(BF16), kb_ref[n % 2], preferred_element_type=F32)
        g = n // n_chunks
        parts[g].append((m_c, l_c, o_c))
        if n % n_chunks == n_chunks - 1:
            merge(g)

    def merge(g):
        tok = slice(g * s_new, (g + 1) * s_new)
        k_new = cnew_ref[tok, :].astype(BF16)
        s_n = _dot_nt(q_lats[g], k_new) + _dot_nt(q_pes[g], knew_ref[tok, :])
        qpos = lax.broadcasted_iota(jnp.int32, s_n.shape, 0) & (s_new - 1)
        kpos = lax.broadcasted_iota(jnp.int32, s_n.shape, 1)
        s_n = jnp.where(kpos <= qpos, s_n, NEG_INF)
        m_n = jnp.max(s_n, axis=-1, keepdims=True)
        p_n = jnp.exp(s_n - m_n)
        parts[g].append((m_n, jnp.sum(p_n, axis=-1, keepdims=True),
                         jnp.dot(p_n.astype(BF16), k_new, preferred_element_type=F32)))
        m = parts[g][0][0]
        for m_c, _, _ in parts[g][1:]:
            m = jnp.maximum(m, m_c)
        l = jnp.zeros((rows, 1), F32)
        acc = jnp.zeros((rows, r), F32)
        for m_c, l_c, o_c in parts[g]:
            w = jnp.exp(m_c - m)
            l = l + w * l_c
            acc = acc + w * o_c
        yb_ref[tok, :] = _value_up(acc / l, wuv_ref, s_new).astype(yb_ref.dtype)

    for n in range(total):
        slot = n % n_slots
        pltpu.make_async_copy(raw_k.at[slot], raw_k.at[slot], sems.at[0, slot]).wait()
        pltpu.make_async_copy(raw_p.at[slot], raw_p.at[slot], sems.at[1, slot]).wait()
        nxt = n + ahead
        if nxt < total:
            start_chunk(nxt, nxt % n_slots)
        else:
            @pl.when(t + 1 < pl.num_programs(0))
            def _():
                start_chunk(nxt, nxt % n_slots)
        k = raw_k[slot].astype(BF16)
        kb_ref[n % 2] = k
        g = n // n_chunks
        s_ref[n % 2] = (_dot_nt(q_lats[g], k)
                        + jnp.dot(q_pes[g], raw_p[slot].astype(BF16), preferred_element_type=F32))
        if n:
            finish(n - 1)
    finish(total - 1)


def _paged_attention(q_cat, ckv_new, kpe_new, cache_ckv, cache_kpe_t, page_table, layer, w_uv_pad, s_new):
    n_h, m, d_qk = q_cat.shape
    r = d_qk - ROPE_DIM
    b, n_pages = page_table.shape
    page = cache_ckv.shape[2]
    pages_chunk = min(PAGES_PER_CHUNK, n_pages)
    n_chunks = n_pages // pages_chunk
    n_slots = PAGED_SLOTS
    g = PAGED_SEQS_PER_STEP
    assert s_new & (s_new - 1) == 0 and n_pages % pages_chunk == 0
    assert b % g == 0 and (g * n_chunks) % n_slots == 0
    keys_chunk = pages_chunk * page
    rows = n_h * s_new
    kern = functools.partial(_paged_attn_kernel, layer=layer, n_chunks=n_chunks, n_slots=n_slots,
                             pages_chunk=pages_chunk, page=page, s_new=s_new, r=r, seqs_step=g)
    grid_spec = pltpu.PrefetchScalarGridSpec(
        num_scalar_prefetch=1, grid=(b // g,),
        in_specs=[pl.BlockSpec((n_h, g * s_new, d_qk), lambda bb, pt: (0, bb, 0)),
                  pl.BlockSpec((g * s_new, r), lambda bb, pt: (bb, 0)),
                  pl.BlockSpec((g * s_new, ROPE_DIM), lambda bb, pt: (bb, 0)),
                  pl.BlockSpec(w_uv_pad.shape, lambda bb, pt: (0, 0, 0)),
                  pl.BlockSpec(memory_space=pl.ANY),
                  pl.BlockSpec(memory_space=pl.ANY)],
        out_specs=pl.BlockSpec((g * s_new, n_h * V_DIM), lambda bb, pt: (bb, 0)),
        scratch_shapes=[pltpu.VMEM((n_slots, keys_chunk, r), F32),
                        pltpu.VMEM((n_slots, ROPE_DIM, keys_chunk), F32),
                        pltpu.VMEM((2, keys_chunk, r), BF16),
                        pltpu.VMEM((2, rows, keys_chunk), F32),
                        pltpu.SemaphoreType.DMA((2, n_slots))])
    return pl.pallas_call(
        kern, grid_spec=grid_spec,
        out_shape=jax.ShapeDtypeStruct((m, n_h * V_DIM), F32),
        compiler_params=_params("arbitrary"),
    )(page_table, q_cat, ckv_new, kpe_new, w_uv_pad, cache_ckv, cache_kpe_t)


def _cd_in_kernel(x_ref, sh_ref, sc_ref, g_ref, w_ref, cw_ref, cb_ref, lcg_ref, lcb_ref, lvg_ref, lvb_ref,
                  hist_ref, wsp_ref, bsp_ref, *rest, bt, ts, d_c, with_v):
    if with_v:
        yc_ref, yd_ref, nconv_ref, vn_ref, hs_ref = rest
    else:
        yc_ref, yd_ref, nconv_ref, hs_ref = rest
    tm = bt * ts
    hist_rows = nconv_ref.shape[1]
    n_taps = hist_rows + 1
    pad = hs_ref.shape[1] - ts
    base = pad - hist_rows
    x = x_ref[...]
    h = _rms(x, g_ref[...]) * (1.0 + sc_ref[...]) + sh_ref[...]
    proj = _dot(h.reshape(tm, h.shape[-1]), w_ref[...])

    glu = (proj[:, 0:d_c] * jax.nn.sigmoid(proj[:, d_c:2 * d_c])).reshape(bt, ts, d_c)

    @pl.when(pl.program_id(1) == 0)
    def _():
        if base:
            hs_ref[:, 0:base, :] = jnp.zeros((bt, base, d_c), F32)
        hs_ref[:, base:pad, :] = hist_ref[...]

    hs_ref[:, pad:pad + ts, :] = glu
    cw = cw_ref[...]
    conv = glu * cw[n_taps - 1:n_taps, :] + cb_ref[...]
    for rem in range(SUBLANES):
        part = None
        for row0 in range(0, pad, SUBLANES):
            k = row0 + rem - base
            if 0 <= k < n_taps - 1:
                term = hs_ref[:, row0:row0 + ts + SUBLANES, :] * cw[k:k + 1, :]
                part = term if part is None else part + term
        if part is not None:
            conv = conv + part[:, rem:rem + ts, :]
    yc = _silu(_ln(conv, lcg_ref[...], lcb_ref[...]))
    yc_ref[...] = yc.reshape(tm, d_c).astype(yc_ref.dtype)
    tail = hs_ref[:, ts + base:ts + pad, :]
    nconv_ref[...] = tail
    hs_ref[:, base:pad, :] = tail

    u = proj[:, 2 * d_c:3 * d_c]
    vn = _ln(proj[:, 3 * d_c:4 * d_c], lvg_ref[...], lvb_ref[...])
    if with_v:
        vn_ref[...] = vn
    low = lax.broadcasted_iota(jnp.int32, (CHUNK, LANES), 1) < (LANES // 2)
    bsp = bsp_ref[...]
    for c in range(tm // CHUNK):
        parts = []
        for j in range(d_c // LANES):
            vj = vn[c * CHUNK:(c + 1) * CHUNK, j * LANES:(j + 1) * LANES]
            rhs = jnp.concatenate([jnp.where(low, vj, 0.0), jnp.where(low, 0.0, vj)], axis=0)
            parts.append(_dot(wsp_ref[j], rhs))
        sv = jnp.concatenate(parts, axis=-1) + bsp
        yd_ref[c * CHUNK:(c + 1) * CHUNK, :] = (u[c * CHUNK:(c + 1) * CHUNK] * sv).astype(yd_ref.dtype)


def _cd_in(x, mod3, g_pre, w_in, conv_w, conv_b, ln_c_g, ln_c_b, ln_v_g, ln_v_b, hist, w_sp_cat, b_sp_rows,
           with_v):
    b, s, d = x.shape
    m = b * s
    bt, ts = _tiles(b, s, IN_ROW_TILE if s >= IN_ROW_TILE else ROW_TILE)
    tm = bt * ts
    d_c = conv_w.shape[1]
    hist_rows = hist.shape[1]
    pad = -(-hist_rows // SUBLANES) * SUBLANES
    grid = (b // bt, s // ts)
    row = lambda bb, ss: (bb * (s // ts) + ss, 0)
    vec = lambda a: a.reshape(1, -1)
    kern = functools.partial(_cd_in_kernel, bt=bt, ts=ts, d_c=d_c, with_v=with_v)
    out_shape = [jax.ShapeDtypeStruct((m, d_c), BF16), jax.ShapeDtypeStruct((m, d_c), BF16),
                 jax.ShapeDtypeStruct((b, hist_rows, d_c), F32)]
    out_specs = [pl.BlockSpec((tm, d_c), row), pl.BlockSpec((tm, d_c), row),
                 pl.BlockSpec((bt, hist_rows, d_c), lambda bb, ss: (bb, 0, 0))]
    if with_v:
        out_shape.append(jax.ShapeDtypeStruct((m, d_c), F32))
        out_specs.append(pl.BlockSpec((tm, d_c), row))
    in_specs = [
        pl.BlockSpec((bt, ts, d), lambda bb, ss: (bb, ss, 0)),
        _mod_spec(bt, d, 0, 2), _mod_spec(bt, d, 1, 2),
        _const_spec((1, d), 2),
        _const_spec(w_in.shape, 2),
        _const_spec(conv_w.shape, 2),
        _const_spec((1, d_c), 2), _const_spec((1, d_c), 2), _const_spec((1, d_c), 2),
        _const_spec((1, d_c), 2), _const_spec((1, d_c), 2),
        pl.BlockSpec((bt, hist_rows, d_c), lambda bb, ss: (bb, 0, 0)),
        _const_spec(w_sp_cat.shape, 2),
        _const_spec(b_sp_rows.shape, 2),
    ]
    return pl.pallas_call(
        kern, grid=grid, in_specs=in_specs, out_specs=tuple(out_specs), out_shape=tuple(out_shape),
        scratch_shapes=[pltpu.VMEM((bt, ts + pad, d_c), F32)],
        compiler_params=_params("arbitrary", "arbitrary"),
    )(x, mod3, mod3, vec(g_pre), w_in, conv_w, vec(conv_b), vec(ln_c_g), vec(ln_c_b), vec(ln_v_g),
      vec(ln_v_b), hist, w_sp_cat, b_sp_rows)


def _out_ffn_kernel(y0_ref, y1_ref, x_ref, gm_ref, shf_ref, scf_ref, gf_ref, npost_ref, nfpre_ref, nfpost_ref,
                    wo_ref, wg_ref, wu_ref, wd_ref, o_ref, *, bt, ts, ff_splits):
    tm = bt * ts
    d = x_ref.shape[-1]
    k0 = y0_ref.shape[-1]
    y = _dot(y0_ref[...], wo_ref[0:k0, :]) + _dot(y1_ref[...], wo_ref[k0:, :])
    x1 = x_ref[...] + gm_ref[...] * _rms(y, npost_ref[...]).reshape(bt, ts, d)
    h = (_rms(x1, nfpre_ref[...]) * (1.0 + scf_ref[...]) + shf_ref[...]).reshape(tm, d).astype(BF16)
    ff = None
    for lo, hi in ff_splits:
        a = _silu(_dot(h, wg_ref[:, lo:hi])) * _dot(h, wu_ref[:, lo:hi])
        part = _dot(a, wd_ref[lo:hi, :])
        ff = part if ff is None else ff + part
    o_ref[...] = x1 + gf_ref[...] * _rms(ff, nfpost_ref[...]).reshape(bt, ts, d)


def _out_ffn(y0, y1, x, mod3, norm_post, norm_ffn_pre, norm_ffn_post, w_out, w_gate, w_up, w_down, layer):
    b, s, d = x.shape
    bt, ts = _tiles(b, s)
    tm = bt * ts
    d_ff = w_gate.shape[2]
    assert d_ff % MXU_COLS == 0
    n_tiles = d_ff // MXU_COLS
    cut = (n_tiles + 1) // 2 * MXU_COLS
    ff_splits = ((0, cut), (cut, d_ff)) if cut < d_ff else ((0, d_ff),)
    grid = (b // bt, s // ts)
    row = lambda bb, ss: (bb * (s // ts) + ss, 0)
    vec = lambda a: a.reshape(1, -1)
    once = pl.Buffered(1)
    kern = functools.partial(_out_ffn_kernel, bt=bt, ts=ts, ff_splits=ff_splits)
    in_specs = [
        pl.BlockSpec((tm, y0.shape[1]), row),
        pl.BlockSpec((tm, y1.shape[1]), row),
        pl.BlockSpec((bt, ts, d), lambda bb, ss: (bb, ss, 0)),
        _mod_spec(bt, d, 2, 2), _mod_spec(bt, d, 3, 2), _mod_spec(bt, d, 4, 2), _mod_spec(bt, d, 5, 2),
        _const_spec((1, d), 2), _const_spec((1, d), 2), _const_spec((1, d), 2),
        pl.BlockSpec(w_out.shape, lambda bb, ss: (0, 0), pipeline_mode=once),
        pl.BlockSpec((None, d, d_ff), lambda bb, ss: (layer, 0, 0), pipeline_mode=once),
        pl.BlockSpec((None, d, d_ff), lambda bb, ss: (layer, 0, 0), pipeline_mode=once),
        pl.BlockSpec((None, d_ff, d), lambda bb, ss: (layer, 0, 0), pipeline_mode=once),
    ]
    return pl.pallas_call(
        kern, grid=grid, in_specs=in_specs,
        out_specs=pl.BlockSpec((bt, ts, d), lambda bb, ss: (bb, ss, 0)),
        out_shape=jax.ShapeDtypeStruct((b, s, d), F32),
        compiler_params=_params("arbitrary", "arbitrary"),
    )(y0, y1, x, mod3, mod3, mod3, mod3, vec(norm_post), vec(norm_ffn_pre), vec(norm_ffn_post),
      w_out, w_gate, w_up, w_down)


def _rope_tables(pos):
    inv_freq = ROPE_THETA ** (-jnp.arange(0, ROPE_DIM, 2, dtype=F32) / ROPE_DIM)
    ang = pos.astype(F32)[:, None] * inv_freq[None, :]
    cos, sin = jnp.cos(ang), jnp.sin(ang)
    cosq = jnp.tile(jnp.concatenate([cos, cos], axis=-1), (1, N_HEADS_B))
    sinq = jnp.tile(jnp.concatenate([-sin, sin], axis=-1), (1, N_HEADS_B))
    zeros = jnp.zeros((pos.shape[0], LANES - 2 * ROPE_DIM), F32)
    ropek = jnp.concatenate([cos, cos, -sin, sin, zeros], axis=-1)
    return cosq, sinq, ropek


def _prep_ab_weights(w_in, w_q_up, w_kv_up, d_a, q_rank, kv_rank):
    d = w_in.shape[0]
    o_kpe = 3 * d_a + q_rank + kv_rank
    half = ROPE_DIM // 2
    swapped = jnp.concatenate([w_in[:, o_kpe + half:o_kpe + ROPE_DIM], w_in[:, o_kpe:o_kpe + half]], axis=1)
    w_in_ext = jnp.concatenate(
        [w_in, swapped, jnp.zeros((d, LANES - 2 * ROPE_DIM), w_in.dtype)], axis=1).astype(BF16)
    wq = w_q_up.reshape(q_rank, N_HEADS_B, NOPE_DIM + ROPE_DIM)
    wq_rope = wq[:, :, NOPE_DIM:]
    wq_rope_sw = jnp.concatenate([wq_rope[:, :, half:], wq_rope[:, :, :half]], axis=2)
    w_q_perm = jnp.concatenate([
        wq[:, :, :NOPE_DIM].reshape(q_rank, -1),
        wq_rope.reshape(q_rank, -1),
        wq_rope_sw.reshape(q_rank, -1)], axis=1).astype(BF16)
    w_kv = w_kv_up.reshape(kv_rank, N_HEADS_B, NOPE_DIM + V_DIM)
    w_uk_t = jnp.transpose(w_kv[..., :NOPE_DIM], (1, 2, 0))
    w_uv = jnp.transpose(w_kv[..., NOPE_DIM:], (1, 0, 2))
    odd = (jnp.arange(N_HEADS_B) % 2 == 1)
    zk = jnp.zeros_like(w_uk_t)
    w_uk_pad = jnp.where(odd[:, None, None],
                         jnp.concatenate([zk, w_uk_t], axis=1),
                         jnp.concatenate([w_uk_t, zk], axis=1)).astype(BF16)
    zv = jnp.zeros_like(w_uv)
    w_uv_pad = jnp.where(odd[:, None, None],
                         jnp.concatenate([zv, w_uv], axis=2),
                         jnp.concatenate([w_uv, zv], axis=2)).astype(BF16)
    return w_in_ext, w_q_perm, w_uk_pad, w_uv_pad


def _prep_spatial(w_sp, b_sp, n):
    wm = jnp.tril(w_sp[:, :n, :n])
    reps = CHUNK // n
    eye = jnp.eye(reps, dtype=w_sp.dtype)
    big = jnp.einsum('ab,gts->gatbs', eye, wm).reshape(G_D, CHUNK, CHUNK)
    w_cat = jnp.concatenate([big[0::2], big[1::2]], axis=2).astype(BF16)
    b_rows = jnp.tile(b_sp[:, :n].T, (reps, 1))
    return w_cat, b_rows


def _trunk(x, mods, pos, hist_a, hist_c, attend, chunk_rows, with_v, ql_dtype, p):
    b, s, d = x.shape
    depth = p['w_ffn_gate'].shape[0]
    cosq, sinq, ropek = _rope_tables(pos)
    conv_a, ckvs, kpes, conv_c, vds = [], [], [], [], []
    for l in range(depth):
        mod3 = mods[l].reshape(b, 1, -1)
        i = l // 2
        if l % 2 == 0:
            w_in_ext, w_q_perm, w_uk_pad, w_uv_pad = p['ab'][i]
            ya, q_cat, ckv, kpe, k_cat, nconv = _ab_in(
                x, mod3, p['norm_mix_pre'][l], w_in_ext, p['conv_a_w'][i], hist_a[i], p['q_norm'][i],
                w_q_perm, p['kv_norm'][i], w_uk_pad, cosq, sinq, ropek, ql_dtype)
            y1 = attend(i, q_cat, ckv, kpe, k_cat, w_uv_pad)
            y0 = ya
            w_out = p['w_out_ab'][i]
            conv_a.append(nconv)
            ckvs.append(ckv.reshape(b, s, -1))
            kpes.append(kpe.reshape(b, s, -1))
        else:
            w_sp_cat, b_rows = _prep_spatial(p['w_spatial'][i], p['b_spatial'][i], chunk_rows)
            d_c = p['conv_c_w'].shape[-1]
            b_sp_rows = jnp.repeat(b_rows, d_c // G_D, axis=1)
            outs = _cd_in(x, mod3, p['norm_mix_pre'][l], p['w_in_cd'][i], p['conv_c_w'][i], p['conv_c_b'][i],
                          p['ln_c_g'][i], p['ln_c_b'][i], p['ln_v_g'][i], p['ln_v_b'][i], hist_c[i],
                          w_sp_cat, b_sp_rows, with_v)
            y0, y1, nconv = outs[:3]
            if with_v:
                vds.append(outs[3].reshape(b, s, -1))
            w_out = p['w_out_cd'][i]
            conv_c.append(nconv)
        x = _out_ffn(y0, y1, x, mod3, p['norm_mix_post'][l], p['norm_ffn_pre'][l], p['norm_ffn_post'][l],
                     w_out, p['w_ffn_gate'], p['w_ffn_up'], p['w_ffn_down'], l)
    return x, conv_a, ckvs, kpes, conv_c, vds


def kernel(x_prompt, x_sample, cache_ckv, cache_kpe, state_conv_a, state_conv_c, page_table, c_prompt, c_sample, w_mod, b_mod, norm_mix_pre, norm_mix_post, norm_ffn_pre, norm_ffn_post, w_in_ab, conv_a_w, q_norm, w_q_up, kv_norm, w_kv_up, w_out_ab, w_in_cd, conv_c_w, conv_c_b, ln_c_g, ln_c_b, ln_v_g, ln_v_b, w_spatial, b_spatial, w_out_cd, w_ffn_gate, w_ffn_up, w_ffn_down):
    b_p, s_p, d = x_prompt.shape
    b_s, s_s, _ = x_sample.shape
    n_ab, n_cd = w_in_ab.shape[0], w_in_cd.shape[0]
    d_a = conv_a_w.shape[-1]
    d_c = conv_c_w.shape[-1]
    q_rank, kv_rank = q_norm.shape[-1], kv_norm.shape[-1]

    p = {
        'norm_mix_pre': norm_mix_pre, 'norm_mix_post': norm_mix_post, 'norm_ffn_pre': norm_ffn_pre,
        'norm_ffn_post': norm_ffn_post, 'conv_a_w': conv_a_w, 'q_norm': q_norm, 'kv_norm': kv_norm,
        'w_out_ab': w_out_ab.astype(BF16), 'w_in_cd': w_in_cd.astype(BF16), 'conv_c_w': conv_c_w,
        'conv_c_b': conv_c_b, 'ln_c_g': ln_c_g, 'ln_c_b': ln_c_b, 'ln_v_g': ln_v_g, 'ln_v_b': ln_v_b,
        'w_spatial': w_spatial, 'b_spatial': b_spatial, 'w_out_cd': w_out_cd.astype(BF16),
        'w_ffn_gate': w_ffn_gate.astype(BF16), 'w_ffn_up': w_ffn_up.astype(BF16),
        'w_ffn_down': w_ffn_down.astype(BF16),
        'ab': [_prep_ab_weights(w_in_ab[i], w_q_up[i], w_kv_up[i], d_a, q_rank, kv_rank) for i in range(n_ab)],
    }

    mods = _modulation(jnp.concatenate([c_prompt, c_sample], axis=0), w_mod, b_mod)
    mods_p, mods_s = mods[:, :b_p], mods[:, b_p:]

    zeros_a = jnp.zeros((n_ab, b_p, state_conv_a.shape[2], d_a), F32)
    zeros_c = jnp.zeros((n_cd, b_p, state_conv_c.shape[2], d_c), F32)

    def attend_prompt(i, q_cat, ckv, kpe, k_cat, w_uv_pad):
        return _prompt_attention(q_cat, k_cat, w_uv_pad, b_p, s_p)

    y_p, sa_p, ckv_p, kpe_p, sc_p, _ = _trunk(
        x_prompt, mods_p, jnp.arange(s_p), zeros_a, zeros_c, attend_prompt, CHUNK, False, BF16, p)

    past_len = page_table.shape[1] * PAGE_SIZE

    cache_kpe_t = jnp.swapaxes(cache_kpe, 2, 3)

    def attend_sample(i, q_cat, ckv, kpe, k_cat, w_uv_pad):
        return _paged_attention(q_cat, ckv, kpe, cache_ckv, cache_kpe_t, page_table, i, w_uv_pad, s_s)

    y_s, sa_s, ckv_s, kpe_s, sc_s, vd_s = _trunk(
        x_sample, mods_s, past_len + jnp.arange(s_s), state_conv_a, state_conv_c, attend_sample, s_s,
        True, F32, p)

    st = jnp.stack
    return (y_p, y_s, st(sa_p), st(ckv_p), st(kpe_p), st(sc_p), st(sa_s), st(ckv_s), st(kpe_s), st(sc_s),
            st(vd_s))
```
